```python
import math
import jax, jax.numpy as jnp
from jax import lax
import numpy as np

D_MODEL = 1024
BATCH = 4
SEQ = 4096
DEPTH = 2
DEC_BATCH = 32
DEC_SEQ = 4
PAST_LEN = 8192
PAGE_SIZE = 128

D_MIX = D_MODEL
DA_HEADS = 4
DA_HEAD_DIM = 64
DA_V_DIM = 2 * DA_HEAD_DIM
DA_WIDTH = DA_HEADS * DA_V_DIM
QK_COLS = DA_HEADS * 2 * DA_HEAD_DIM
ROT_DIM = DA_HEAD_DIM // 4
ROPE_THETA = 500000.0
GM_WIDTH = D_MIX - DA_WIDTH
GM_GROUPS = 4
GM_CH = GM_WIDTH // GM_GROUPS
CHUNK = 128
Q_BLOCK = 128
IN_COLS = 2 * QK_COLS + DA_WIDTH + 2 * GM_WIDTH
D_FF = 2816
N_EXPERTS = 8
TOP_K = 2
D_FF_EXPERT = 3584
N_DENSE = (DEPTH + 1) // 2
N_MOE = DEPTH // 2
RMS_EPS = 1e-6
SUBLN_EPS = 1e-5
LN_EPS = 1e-5

kernel_name = 'hybrid_diffattn_gmlp_moe_decode_step'


def rmsnorm(x, g, eps=RMS_EPS):
    xf = x.astype(jnp.float32)
    y = xf * lax.rsqrt(jnp.mean(xf * xf, axis=-1, keepdims=True) + eps)
    return (y * g.astype(jnp.float32)).astype(x.dtype)


def layernorm(x, g, b, eps=LN_EPS):
    xf = x.astype(jnp.float32)
    mu = jnp.mean(xf, axis=-1, keepdims=True)
    var = jnp.mean(jnp.square(xf - mu), axis=-1, keepdims=True)
    return ((xf - mu) * lax.rsqrt(var + eps) * g.astype(jnp.float32) + b.astype(jnp.float32)).astype(x.dtype)


def rope(x, pos):
    inv = ROPE_THETA ** (-jnp.arange(0, ROT_DIM, 2, dtype=jnp.float32) / ROT_DIM)
    ang = pos[:, None] * inv[None, :]
    cos = jnp.cos(ang)[:, None, None, :]
    sin = jnp.sin(ang)[:, None, None, :]
    xr = x[..., :ROT_DIM].astype(jnp.float32)
    x1, x2 = xr[..., :ROT_DIM // 2], xr[..., ROT_DIM // 2:]
    rot = jnp.concatenate([x1 * cos - x2 * sin, x2 * cos + x1 * sin], axis=-1).astype(x.dtype)
    return jnp.concatenate([rot, x[..., ROT_DIM:]], axis=-1)


def project(h, w_in_l):
    z = h @ w_in_l
    B, S = z.shape[:2]
    q, k, v, gz = jnp.split(z, [QK_COLS, 2 * QK_COLS, 2 * QK_COLS + DA_WIDTH], axis=-1)
    q = q.reshape(B, S, DA_HEADS, 2, DA_HEAD_DIM)
    k = k.reshape(B, S, DA_HEADS, 2, DA_HEAD_DIM)
    v = v.reshape(B, S, DA_HEADS, DA_V_DIM)
    return q, k, v, gz


def diff_lambda(lq1, lk1, lq2, lk2, lam_init):
    f32 = lambda a: a.astype(jnp.float32)
    return jnp.exp(jnp.sum(f32(lq1) * f32(lk1))) - jnp.exp(jnp.sum(f32(lq2) * f32(lk2))) + lam_init


def diff_probs(s, mask, lam):
    p = jax.nn.softmax(jnp.where(mask, s, -jnp.inf), axis=-1)
    return p[:, :, 0] - lam * p[:, :, 1]


def attn_prompt(q, k, v, lam):
    B, S = q.shape[:2]
    nb = S // Q_BLOCK
    scale = DA_HEAD_DIM ** -0.5
    qb = q.reshape(B, nb, Q_BLOCK, DA_HEADS, 2, DA_HEAD_DIM).swapaxes(0, 1)
    kpos = jnp.arange(S)

    def block(args):
        qi, i = args
        s = jnp.einsum('bqhcd,bkhcd->bhcqk', qi, k).astype(jnp.float32) * scale
        qpos = i * Q_BLOCK + jnp.arange(Q_BLOCK)
        p = diff_probs(s, kpos[None, :] <= qpos[:, None], lam)
        return jnp.einsum('bhqk,bkhd->bqhd', p.astype(v.dtype), v)

    o = lax.map(block, (qb, jnp.arange(nb)))
    return o.swapaxes(0, 1).reshape(B, S, DA_HEADS, DA_V_DIM)


def attn_sample(q, k, v, k_past, v_past, lam):
    T = q.shape[1]
    P = k_past.shape[1]
    scale = DA_HEAD_DIM ** -0.5
    s_past = jnp.einsum('bqhcd,bkhcd->bhcqk', q, k_past).astype(jnp.float32) * scale
    s_new = jnp.einsum('bqhcd,bkhcd->bhcqk', q, k).astype(jnp.float32) * scale
    s = jnp.concatenate([s_past, s_new], axis=-1)
    mask = jnp.concatenate([jnp.ones((T, P), bool), jnp.tril(jnp.ones((T, T), bool))], axis=-1)
    p = diff_probs(s, mask, lam).astype(v.dtype)
    return (jnp.einsum('bhqk,bkhd->bqhd', p[..., :P], v_past)
            + jnp.einsum('bhqk,bkhd->bqhd', p[..., P:], v))


def sgu_split(gz, ln_g, ln_b):
    B, S = gz.shape[:2]
    gz = jax.nn.gelu(gz, approximate=False)
    u, gv = gz[..., :GM_WIDTH], gz[..., GM_WIDTH:]
    gv = layernorm(gv, ln_g, ln_b)
    return u, gv.reshape(B, S, GM_GROUPS, GM_CH)


def causal_ws(w_s):
    return w_s * jnp.tril(jnp.ones((CHUNK, CHUNK), w_s.dtype))


def spatial_prompt(gv, w_s, b_s):
    B, S = gv.shape[:2]
    vc = gv.reshape(B, S // CHUNK, CHUNK, GM_GROUPS, GM_CH)
    m = jnp.einsum('gts,bnsgc->bntgc', causal_ws(w_s), vc) + b_s.T[:, :, None]
    return m.reshape(B, S, GM_GROUPS, GM_CH)


def spatial_sample(gv, w_s, b_s):
    T = gv.shape[1]
    w = causal_ws(w_s)[:, :T, :T]
    return jnp.einsum('gts,bsgc->btgc', w, gv) + b_s[:, :T].T[:, :, None]


def token_mixer(h, pos, attn_fn, spatial_fn, w_in_l, w_out_l, g_mix, lam, lam_init, subln_g_l, ln_g, ln_b):
    B, S = h.shape[:2]
    q, k, v, gz = project(rmsnorm(h, g_mix), w_in_l)
    q, k = rope(q, pos), rope(k, pos)
    a = rmsnorm(attn_fn(q, k, v, lam), subln_g_l, SUBLN_EPS) * (1.0 - lam_init)
    u, gv = sgu_split(gz, ln_g, ln_b)
    g = u * spatial_fn(gv).reshape(B, S, GM_WIDTH)
    out = jnp.concatenate([a.reshape(B, S, DA_WIDTH), g], axis=-1) @ w_out_l
    return out, k.reshape(B, S, DA_HEADS, 2 * DA_HEAD_DIM), v, gv


def swiglu(x, wg, wu, wd):
    return (jax.nn.silu(x @ wg) * (x @ wu)) @ wd


def moe(x, w_router, wg, wu, wd):
    shp = x.shape
    xt = x.reshape(-1, D_MODEL)
    logits = (xt @ w_router).astype(jnp.float32)
    top_v, top_i = lax.top_k(logits, TOP_K)
    gates = jax.nn.softmax(top_v, axis=-1)
    dense_gate = jnp.sum(jax.nn.one_hot(top_i, N_EXPERTS, dtype=jnp.float32) * gates[..., None], axis=1)
    y = jnp.zeros_like(xt)
    for e in range(N_EXPERTS):
        y = y + dense_gate[:, e:e + 1].astype(xt.dtype) * swiglu(xt, wg[e], wu[e], wd[e])
    return y.reshape(shp)


def setup_inputs(seed: int = 0) -> dict:
    key = jax.random.key(seed)
    ks = jax.random.split(key, 32)
    n_pages = PAST_LEN // PAGE_SIZE
    n_used = DEC_BATCH * n_pages
    n_phys = n_used + max(1, n_used // 4)

    def nrm(k, shape, s):
        return jax.random.normal(k, shape, jnp.float32) * s

    def gain(k, shape):
        return 1.0 + 0.02 * jax.random.normal(k, shape, jnp.float32)

    page_table = jax.random.permutation(ks[4], n_phys)[:n_used].reshape(DEC_BATCH, n_pages).astype(jnp.int32)
    return {
        'x_prompt': nrm(ks[0], (BATCH, SEQ, D_MODEL), 1.0),
        'x_sample': nrm(ks[1], (DEC_BATCH, DEC_SEQ, D_MODEL), 1.0),
        'cache_k': nrm(ks[2], (DEPTH, n_phys, PAGE_SIZE, DA_HEADS, 2 * DA_HEAD_DIM), 1.0),
        'cache_v': nrm(ks[3], (DEPTH, n_phys, PAGE_SIZE, DA_HEADS, DA_V_DIM), 1.0),
        'page_table': page_table,
        'w_in': nrm(ks[5], (DEPTH, D_MODEL, IN_COLS), D_MODEL ** -0.5),
        'w_out': nrm(ks[6], (DEPTH, D_MIX, D_MODEL), D_MIX ** -0.5),
        'norm_mix_g': gain(ks[7], (DEPTH, D_MODEL)),
        'norm_ffn_g': gain(ks[8], (DEPTH, D_MODEL)),
        'lam_q1': nrm(ks[9], (DEPTH, DA_HEAD_DIM), 0.1),
        'lam_k1': nrm(ks[10], (DEPTH, DA_HEAD_DIM), 0.1),
        'lam_q2': nrm(ks[11], (DEPTH, DA_HEAD_DIM), 0.1),
        'lam_k2': nrm(ks[12], (DEPTH, DA_HEAD_DIM), 0.1),
        'subln_g': gain(ks[13], (DEPTH, DA_V_DIM)),
        'gm_ln_g': gain(ks[14], (DEPTH, GM_WIDTH)),
        'gm_ln_b': nrm(ks[15], (DEPTH, GM_WIDTH), 0.02),
        'gm_ws': nrm(ks[16], (DEPTH, GM_GROUPS, CHUNK, CHUNK), CHUNK ** -0.5),
        'gm_bs': gain(ks[17], (DEPTH, GM_GROUPS, CHUNK)),
        'ffn_wg': nrm(ks[18], (N_DENSE, D_MODEL, D_FF), D_MODEL ** -0.5),
        'ffn_wu': nrm(ks[19], (N_DENSE, D_MODEL, D_FF), D_MODEL ** -0.5),
        'ffn_wd': nrm(ks[20], (N_DENSE, D_FF, D_MODEL), D_FF ** -0.5),
        'moe_router': nrm(ks[21], (N_MOE, D_MODEL, N_EXPERTS), D_MODEL ** -0.5),
        'moe_wg': nrm(ks[22], (N_MOE, N_EXPERTS, D_MODEL, D_FF_EXPERT), D_MODEL ** -0.5),
        'moe_wu': nrm(ks[23], (N_MOE, N_EXPERTS, D_MODEL, D_FF_EXPERT), D_MODEL ** -0.5),
        'moe_wd': nrm(ks[24], (N_MOE, N_EXPERTS, D_FF_EXPERT, D_MODEL), D_FF_EXPERT ** -0.5),
        'final_norm_g': gain(ks[25], (D_MODEL,)),
    }


def reference(x_prompt, x_sample, cache_k, cache_v, page_table, w_in, w_out, norm_mix_g, norm_ffn_g,
              lam_q1, lam_k1, lam_q2, lam_k2, subln_g, gm_ln_g, gm_ln_b, gm_ws, gm_bs,
              ffn_wg, ffn_wu, ffn_wd, moe_router, moe_wg, moe_wu, moe_wd, final_norm_g):
    S = x_prompt.shape[1]
    DB, T = x_sample.shape[:2]
    n_pages = page_table.shape[1]
    past = n_pages * PAGE_SIZE
    pos_p = jnp.arange(S, dtype=jnp.float32)
    pos_s = jnp.arange(T, dtype=jnp.float32) + jnp.float32(past)

    hp, hs = x_prompt, x_sample
    kp_rows, vp_rows, ks_rows, vs_rows, gv_rows = [], [], [], [], []
    for l in range(DEPTH):
        lam_init = 0.8 - 0.6 * math.exp(-0.3 * l)
        lam = diff_lambda(lam_q1[l], lam_k1[l], lam_q2[l], lam_k2[l], lam_init)
        ws_l, bs_l = gm_ws[l], gm_bs[l]

        dp, kp, vp, _ = token_mixer(
            hp, pos_p, attn_prompt, lambda gv: spatial_prompt(gv, ws_l, bs_l),
            w_in[l], w_out[l], norm_mix_g[l], lam, lam_init, subln_g[l], gm_ln_g[l], gm_ln_b[l])
        hp = hp + dp

        k_past = cache_k[l][page_table].reshape(DB, past, DA_HEADS, 2, DA_HEAD_DIM)
        v_past = cache_v[l][page_table].reshape(DB, past, DA_HEADS, DA_V_DIM)
        ds, ks_new, vs_new, gvs = token_mixer(
            hs, pos_s, lambda q, k, v, lm: attn_sample(q, k, v, k_past, v_past, lm),
            lambda gv: spatial_sample(gv, ws_l, bs_l),
            w_in[l], w_out[l], norm_mix_g[l], lam, lam_init, subln_g[l], gm_ln_g[l], gm_ln_b[l])
        hs = hs + ds

        if l % 2 == 0:
            j = l // 2
            hp = hp + swiglu(rmsnorm(hp, norm_ffn_g[l]), ffn_wg[j], ffn_wu[j], ffn_wd[j])
            hs = hs + swiglu(rmsnorm(hs, norm_ffn_g[l]), ffn_wg[j], ffn_wu[j], ffn_wd[j])
        else:
            j = l // 2
            hp = hp + moe(rmsnorm(hp, norm_ffn_g[l]), moe_router[j], moe_wg[j], moe_wu[j], moe_wd[j])
            hs = hs + moe(rmsnorm(hs, norm_ffn_g[l]), moe_router[j], moe_wg[j], moe_wu[j], moe_wd[j])

        kp_rows.append(kp)
        vp_rows.append(vp)
        ks_rows.append(ks_new)
        vs_rows.append(vs_new)
        gv_rows.append(gvs)

    y_prompt = rmsnorm(hp, final_norm_g)
    y_sample = rmsnorm(hs, final_norm_g)
    return (y_prompt, y_sample, jnp.stack(kp_rows), jnp.stack(vp_rows),
            jnp.stack(ks_rows), jnp.stack(vs_rows), jnp.stack(gv_rows))
```

```python
import functools
import math

import jax
import jax.numpy as jnp
from jax import lax
from jax.experimental import pallas as pl
from jax.experimental.pallas import tpu as pltpu

F32 = jnp.float32
BF16 = jnp.bfloat16

DA_HEADS = 4
DA_HEAD_DIM = 64
DA_V_DIM = 2 * DA_HEAD_DIM
QK_COLS = DA_HEADS * 2 * DA_HEAD_DIM
DA_WIDTH = DA_HEADS * DA_V_DIM
ROT_DIM = DA_HEAD_DIM // 4
ROPE_THETA = 500000.0
GM_GROUPS = 4
GM_CH = 128
GM_WIDTH = GM_GROUPS * GM_CH
CHUNK = 128
PAGE_SIZE = 128
TOP_K = 2
RMS_EPS = 1e-6
SUBLN_EPS = 1e-5
LN_EPS = 1e-5
NEG_BIG = -1e30
LANES = 128

VMEM_LIMIT = 52 * 1024 * 1024


def _cparams(sem):
    return pltpu.CompilerParams(dimension_semantics=sem, vmem_limit_bytes=VMEM_LIMIT)


def _nt_dot(a, b):
    return lax.dot_general(a, b, (((1,), (1,)), ((), ())), preferred_element_type=F32)


def _rmsnorm_rows(x, g, eps):
    return x * lax.rsqrt(jnp.mean(x * x, axis=-1, keepdims=True) + eps) * g


def _gelu_exact(x):
    return 0.5 * x * (1.0 + lax.erf(x * (2.0 ** -0.5)))


def _diff_lambda(lq1, lk1, lq2, lk2, lam_init):
    a = jnp.exp(jnp.sum(lq1 * lk1, axis=-1, keepdims=True))
    b = jnp.exp(jnp.sum(lq2 * lk2, axis=-1, keepdims=True))
    return a - b + lam_init


def _proj_body(h_ref, g_ref, w_ref, cos_ref, sa_ref, sb_ref, lng_ref, lnb_ref,
               q_ref, k_ref, v_ref, kb_ref, vb_ref, u_ref, gv_ref):
    h = h_ref[...]
    xn = _rmsnorm_rows(h, g_ref[...], RMS_EPS).astype(BF16)
    z = jnp.dot(xn, w_ref[...], preferred_element_type=F32)
    cos, sa, sb = cos_ref[...], sa_ref[...], sb_ref[...]

    def rope(x):
        return (x * cos + pltpu.roll(x, ROT_DIM // 2, 1) * sa
                + pltpu.roll(x, LANES - ROT_DIM // 2, 1) * sb)

    scale = DA_HEAD_DIM ** -0.5
    for t in range(DA_HEADS):
        sl = slice(t * LANES, (t + 1) * LANES)
        q_ref[:, sl] = (rope(z[:, sl]) * scale).astype(q_ref.dtype)
        kt = rope(z[:, QK_COLS + t * LANES:QK_COLS + (t + 1) * LANES])
        k_ref[:, sl] = kt
        kb_ref[:, sl] = kt.astype(kb_ref.dtype)
    v = z[:, 2 * QK_COLS:2 * QK_COLS + DA_WIDTH]
    v_ref[...] = v
    vb_ref[...] = v.astype(vb_ref.dtype)
    gz = _gelu_exact(z[:, 2 * QK_COLS + DA_WIDTH:])
    u_ref[...] = gz[:, :GM_WIDTH].astype(u_ref.dtype)
    gvr = gz[:, GM_WIDTH:]
    mu = jnp.mean(gvr, axis=-1, keepdims=True)
    d = gvr - mu
    var = jnp.mean(d * d, axis=-1, keepdims=True)
    gv = d * lax.rsqrt(var + LN_EPS) * lng_ref[...] + lnb_ref[...]
    gv_ref[...] = gv.astype(gv_ref.dtype)


def _proj(h, g, w_bf, tables, lng, lnb, tm, u_dtype, gv_dtype):
    m, d = h.shape
    cos_t, sa_t, sb_t = tables
    ntab = cos_t.shape[0] // tm
    row = lambda i: (i, 0)
    fixed = lambda i: (0, 0)
    tab = lambda i: (i % ntab, 0)
    wide = lambda dt: jax.ShapeDtypeStruct((m, QK_COLS), dt)
    return pl.pallas_call(
        _proj_body,
        grid=(m // tm,),
        in_specs=[
            pl.BlockSpec((tm, d), row),
            pl.BlockSpec((1, d), fixed),
            pl.BlockSpec(w_bf.shape, fixed),
            pl.BlockSpec((tm, LANES), tab),
            pl.BlockSpec((tm, LANES), tab),
            pl.BlockSpec((tm, LANES), tab),
            pl.BlockSpec((1, GM_WIDTH), fixed),
            pl.BlockSpec((1, GM_WIDTH), fixed),
        ],
        out_specs=[pl.BlockSpec((tm, QK_COLS), row)] * 7,
        out_shape=[wide(BF16), wide(F32), wide(F32), wide(BF16), wide(BF16),
                   wide(u_dtype), wide(gv_dtype)],
        compiler_params=_cparams(("parallel",)),
        name="proj",
    )(h, g, w_bf, cos_t, sa_t, sb_t, lng, lnb)


def _rope_tables(pos):
    inv = ROPE_THETA ** (-jnp.arange(0, ROT_DIM, 2, dtype=F32) / ROT_DIM)
    ang = pos[:, None] * inv[None, :]
    c, s = jnp.cos(ang), jnp.sin(ang)
    j = jnp.arange(LANES) % DA_HEAD_DIM
    first = j < ROT_DIM // 2
    second = (j >= ROT_DIM // 2) & (j < ROT_DIM)
    idx = j % (ROT_DIM // 2)
    cg, sg = c[:, idx], s[:, idx]
    cos_t = jnp.where(first | second, cg, 1.0)
    sa_t = jnp.where(second, sg, 0.0)
    sb_t = jnp.where(first, -sg, 0.0)
    return cos_t, sa_t, sb_t


def _attn_prompt_body(lam_init, blk, lq1_ref, lk1_ref, lq2_ref, lk2_ref, sg_ref,
                      q_ref, k_ref, v_ref, o_ref):
    qi = pl.program_id(2)
    q = q_ref[0]
    lane = lax.broadcasted_iota(jnp.int32, q.shape, 1)
    zero = jnp.zeros_like(q)
    qs = (jnp.where(lane < DA_HEAD_DIM, q, zero), jnp.where(lane >= DA_HEAD_DIM, q, zero))

    def update(state, kb, vb, mask):
        new = []
        for c in range(2):
            m, l, a = state[3 * c:3 * c + 3]
            s = _nt_dot(qs[c], kb)
            if mask is not None:
                s = jnp.where(mask, s, NEG_BIG)
            mn = jnp.maximum(m, jnp.max(s, axis=-1, keepdims=True))
            p = jnp.exp(s - mn)
            alpha = jnp.exp(m - mn)
            l = alpha * l + jnp.sum(p, axis=-1, keepdims=True)
            a = alpha * a + jnp.dot(p.astype(BF16), vb, preferred_element_type=F32)
            new += [mn, l, a]
        return tuple(new)

    def body(j, state):
        start = pl.multiple_of(j * blk, blk)
        return update(state, k_ref[0, pl.ds(start, blk), :], v_ref[0, pl.ds(start, blk), :], None)

    col = jnp.full((blk, 1), NEG_BIG, F32)
    init = (col, jnp.zeros((blk, 1), F32), jnp.zeros((blk, DA_V_DIM), F32)) * 2
    state = lax.fori_loop(0, qi, body, init)
    start = pl.multiple_of(qi * blk, blk)
    r = lax.broadcasted_iota(jnp.int32, (blk, blk), 0)
    cidx = lax.broadcasted_iota(jnp.int32, (blk, blk), 1)
    state = update(state, k_ref[0, pl.ds(start, blk), :], v_ref[0, pl.ds(start, blk), :], cidx <= r)
    lam = _diff_lambda(lq1_ref[...], lk1_ref[...], lq2_ref[...], lk2_ref[...], lam_init)
    o = state[2] / state[1] - lam * (state[5] / state[4])
    o = _rmsnorm_rows(o, sg_ref[...], SUBLN_EPS) * (1.0 - lam_init)
    o_ref[0] = o.astype(o_ref.dtype)


def _attn_prompt(q, kb, vb, lams, sg, lam_init, blk):
    b, s, _ = q.shape
    small = pl.BlockSpec((1, DA_HEAD_DIM), lambda bi, h, i: (0, 0))
    return pl.pallas_call(
        functools.partial(_attn_prompt_body, lam_init, blk),
        grid=(b, DA_HEADS, s // blk),
        in_specs=[small, small, small, small,
                  pl.BlockSpec((1, DA_V_DIM), lambda bi, h, i: (0, 0)),
                  pl.BlockSpec((1, blk, LANES), lambda bi, h, i: (bi, i, h)),
                  pl.BlockSpec((1, s, LANES), lambda bi, h, i: (bi, 0, h)),
                  pl.BlockSpec((1, s, LANES), lambda bi, h, i: (bi, 0, h))],
        out_specs=pl.BlockSpec((1, blk, LANES), lambda bi, h, i: (bi, i, h)),
        out_shape=jax.ShapeDtypeStruct((b, s, DA_WIDTH), BF16),
        compiler_params=_cparams(("parallel", "parallel", "arbitrary")),
        name="attn_prompt",
    )(*lams, sg, q, kb, vb)


def _attn_sample_body(lam_init, n_tok, pages_per_step, pt_ref, lq1_ref, lk1_ref, lq2_ref, lk2_ref,
                      sg_ref, q_ref, kn_ref, vn_ref, *rest):
    kp = rest[:pages_per_step]
    vp = rest[pages_per_step:2 * pages_per_step]
    o_ref = rest[2 * pages_per_step]
    m_sc, l_sc, a_sc = rest[2 * pages_per_step + 1:]
    j = pl.program_id(1)
    nrow = DA_HEADS * 2 * n_tok
    ncol = PAGE_SIZE * DA_HEADS

    @pl.when(j == 0)
    def _():
        m_sc[...] = jnp.full(m_sc.shape, NEG_BIG, F32)
        l_sc[...] = jnp.zeros(l_sc.shape, F32)
        a_sc[...] = jnp.zeros(a_sc.shape, F32)

    q = q_ref[0]
    row = lax.broadcasted_iota(jnp.int32, (nrow, ncol), 0)
    colk = lax.broadcasted_iota(jnp.int32, (nrow, ncol), 1)
    same_head = (colk % DA_HEADS) == (row // (2 * n_tok))

    def update(s_list, v_list):
        m = m_sc[...]
        mn = m
        for s in s_list:
            mn = jnp.maximum(mn, jnp.max(s, axis=-1, keepdims=True))
        alpha = jnp.exp(m - mn)
        l = alpha * l_sc[...]
        a = alpha * a_sc[...]
        for s, v in zip(s_list, v_list):
            p = jnp.exp(s - mn)
            l = l + jnp.sum(p, axis=-1, keepdims=True)
            a = a + jnp.dot(p.astype(BF16), v, preferred_element_type=F32)
        m_sc[...] = mn
        l_sc[...] = l
        a_sc[...] = a

    s_list, v_list = [], []
    for i in range(pages_per_step):
        kb = kp[i][0, 0].astype(BF16)
        s_list.append(jnp.where(same_head, _nt_dot(q, kb), NEG_BIG))
        v_list.append(vp[i][0, 0].astype(BF16))
    update(s_list, v_list)

    @pl.when(j == pl.num_programs(1) - 1)
    def _():
        rown = lax.broadcasted_iota(jnp.int32, (nrow, kn_ref.shape[1]), 0)
        coln = lax.broadcasted_iota(jnp.int32, (nrow, kn_ref.shape[1]), 1)
        ok = ((coln < n_tok * DA_HEADS) & ((coln % DA_HEADS) == (rown // (2 * n_tok)))
              & ((coln // DA_HEADS) <= (rown % n_tok)))
        s = jnp.where(ok, _nt_dot(q, kn_ref[0]), NEG_BIG)
        update([s], [vn_ref[0]])
        a = a_sc[...] / l_sc[...]
        lam = _diff_lambda(lq1_ref[...], lk1_ref[...], lq2_ref[...], lk2_ref[...], lam_init)
        for h in range(DA_HEADS):
            base = h * 2 * n_tok
            o = a[base:base + n_tok] - lam * a[base + n_tok:base + 2 * n_tok]
            o = _rmsnorm_rows(o, sg_ref[...], SUBLN_EPS) * (1.0 - lam_init)
            o_ref[0, :, h * LANES:(h + 1) * LANES] = o


def _attn_sample(layer, q_rows, kn, vn, cache_k, cache_v, page_table, lams, sg, lam_init,
                 n_tok, pages_per_step):
    db, nrow, _ = q_rows.shape
    n_pages = page_table.shape[1]
    assert n_pages % pages_per_step == 0
    pt = page_table.reshape(-1)
    small = pl.BlockSpec((1, DA_HEAD_DIM), lambda b, j, p: (0, 0))
    per_seq = lambda b, j, p: (b, 0, 0)

    def page_spec(i):
        return pl.BlockSpec(
            (1, 1, PAGE_SIZE * DA_HEADS, LANES),
            lambda b, j, p: (layer, p[b * n_pages + j * pages_per_step + i], 0, 0))

    pages = [page_spec(i) for i in range(pages_per_step)]
    grid_spec = pltpu.PrefetchScalarGridSpec(
        num_scalar_prefetch=1,
        grid=(db, n_pages // pages_per_step),
        in_specs=[small, small, small, small,
                  pl.BlockSpec((1, DA_V_DIM), lambda b, j, p: (0, 0)),
                  pl.BlockSpec((1, nrow, LANES), per_seq),
                  pl.BlockSpec((1,) + kn.shape[1:], per_seq),
                  pl.BlockSpec((1,) + vn.shape[1:], per_seq)] + pages + pages,
        out_specs=pl.BlockSpec((1, n_tok, DA_WIDTH), per_seq),
        scratch_shapes=[pltpu.VMEM((nrow, 1), F32), pltpu.VMEM((nrow, 1), F32),
                        pltpu.VMEM((nrow, DA_V_DIM), F32)],
    )
    return pl.pallas_call(
        functools.partial(_attn_sample_body, lam_init, n_tok, pages_per_step),
        grid_spec=grid_spec,
        out_shape=jax.ShapeDtypeStruct((db, n_tok, DA_WIDTH), F32),
        compiler_params=_cparams(("parallel", "arbitrary")),
        name="attn_sample",
    )(pt, *lams, sg, q_rows, kn, vn, *([cache_k] * pages_per_step), *([cache_v] * pages_per_step))


def _mix_out_prompt_body(n_chunk, a_ref, u_ref, gv_ref, h_ref, ws_ref, bst_ref, wo_ref, gf_ref,
                         ho_ref, xn_ref, g_sc):
    r = lax.broadcasted_iota(jnp.int32, (CHUNK, CHUNK), 0)
    c = lax.broadcasted_iota(jnp.int32, (CHUNK, CHUNK), 1)
    for g in range(GM_GROUPS):
        w = jnp.where(c <= r, ws_ref[g], 0.0).astype(BF16)
        bias = bst_ref[:, g:g + 1]
        lanes = slice(g * GM_CH, (g + 1) * GM_CH)
        for ci in range(n_chunk):
            rows = slice(ci * CHUNK, (ci + 1) * CHUNK)
            m = jnp.dot(w, gv_ref[rows, lanes], preferred_element_type=F32) + bias
            g_sc[rows, lanes] = (u_ref[rows, lanes].astype(F32) * m).astype(BF16)
    out = (jnp.dot(a_ref[...], wo_ref[:DA_WIDTH, :], preferred_element_type=F32)
           + jnp.dot(g_sc[...], wo_ref[DA_WIDTH:, :], preferred_element_type=F32))
    hn = h_ref[...] + out
    ho_ref[...] = hn
    xn_ref[...] = _rmsnorm_rows(hn, gf_ref[...], RMS_EPS)


def _mix_out_sample_body(n_tok, n_seq, a_ref, u_ref, gv_ref, h_ref, ws_ref, bs_ref, wo_ref, gf_ref,
                         ho_ref, xn_ref, g_sc):
    for g in range(GM_GROUPS):
        lanes = slice(g * GM_CH, (g + 1) * GM_CH)
        for t in range(n_tok):
            m = jnp.zeros((n_seq, GM_CH), F32) + bs_ref[g:g + 1, t:t + 1]
            for s in range(t + 1):
                m = m + ws_ref[g, t:t + 1, s:s + 1] * gv_ref[s * n_seq:(s + 1) * n_seq, lanes]
            rows = slice(t * n_seq, (t + 1) * n_seq)
            g_sc[rows, lanes] = u_ref[rows, lanes] * m
    out = (jnp.dot(a_ref[...], wo_ref[:DA_WIDTH, :], preferred_element_type=F32)
           + jnp.dot(g_sc[...].astype(BF16), wo_ref[DA_WIDTH:, :], preferred_element_type=F32))
    hn = h_ref[...] + out
    ho_ref[...] = hn
    xn_ref[...] = _rmsnorm_rows(hn, gf_ref[...], RMS_EPS)


def _mix_out(a, u, gv, h, ws, bs, wo_bf, gf, tm, sample_shape=None):
    m, d = h.shape
    row = lambda i: (i, 0)
    fixed2 = lambda i: (0, 0)
    fixed3 = lambda i: (0, 0, 0)
    if sample_shape is None:
        body = functools.partial(_mix_out_prompt_body, tm // CHUNK)
        bias = bs.T
        g_dtype = BF16
    else:
        body = functools.partial(_mix_out_sample_body, *sample_shape)
        bias = bs
        g_dtype = F32
    return pl.pallas_call(
        body,
        grid=(m // tm,),
        in_specs=[pl.BlockSpec((tm, DA_WIDTH), row),
                  pl.BlockSpec((tm, GM_WIDTH), row),
                  pl.BlockSpec((tm, GM_WIDTH), row),
                  pl.BlockSpec((tm, d), row),
                  pl.BlockSpec(ws.shape, fixed3),
                  pl.BlockSpec(bias.shape, fixed2),
                  pl.BlockSpec(wo_bf.shape, fixed2),
                  pl.BlockSpec((1, d), fixed2)],
        out_specs=[pl.BlockSpec((tm, d), row), pl.BlockSpec((tm, d), row)],
        out_shape=[jax.ShapeDtypeStruct((m, d), F32), jax.ShapeDtypeStruct((m, d), F32)],
        scratch_shapes=[pltpu.VMEM((tm, GM_WIDTH), g_dtype)],
        compiler_params=_cparams(("parallel",)),
        name="mix_out",
    )(a, u, gv, h, ws, bias, wo_bf, gf)


def _ffn_body(xn_ref, h_ref, wg_ref, wu_ref, wd_ref, o_ref, acc_sc):
    f = pl.program_id(1)

    @pl.when(f == 0)
    def _():
        acc_sc[...] = jnp.zeros(acc_sc.shape, F32)

    x = xn_ref[...].astype(BF16)
    g = jnp.dot(x, wg_ref[...], preferred_element_type=F32)
    u = jnp.dot(x, wu_ref[...], preferred_element_type=F32)
    hid = (g * jax.nn.sigmoid(g) * u).astype(BF16)
    acc_sc[...] += jnp.dot(hid, wd_ref[...], preferred_element_type=F32)

    @pl.when(f == pl.num_programs(1) - 1)
    def _():
        o_ref[...] = h_ref[...] + acc_sc[...]


def _ffn(xn, h, wg, wu, wd, tm, tf):
    m, d = h.shape
    ff = wg.shape[1]
    return pl.pallas_call(
        _ffn_body,
        grid=(m // tm, ff // tf),
        in_specs=[pl.BlockSpec((tm, d), lambda i, f: (i, 0)),
                  pl.BlockSpec((tm, d), lambda i, f: (i, 0)),
                  pl.BlockSpec((d, tf), lambda i, f: (0, f)),
                  pl.BlockSpec((d, tf), lambda i, f: (0, f)),
                  pl.BlockSpec((tf, d), lambda i, f: (f, 0))],
        out_specs=pl.BlockSpec((tm, d), lambda i, f: (i, 0)),
        out_shape=jax.ShapeDtypeStruct((m, d), F32),
        scratch_shapes=[pltpu.VMEM((tm, d), F32)],
        compiler_params=_cparams(("parallel", "arbitrary")),
        name="ffn",
    )(xn, h, wg, wu, wd)


def _router_body(n_exp, xn_ref, wr_ref, o_ref):
    logits = jnp.dot(xn_ref[...], wr_ref[...], preferred_element_type=F32,
                     precision=lax.Precision.HIGHEST)
    col = lax.broadcasted_iota(jnp.int32, logits.shape, 1)
    logits = jnp.where(col < n_exp, logits, -jnp.inf)
    big = jnp.int32(LANES)
    v1 = jnp.max(logits, axis=-1, keepdims=True)
    i1 = jnp.min(jnp.where(logits == v1, col, big), axis=-1, keepdims=True)
    rest = jnp.where(col == i1, -jnp.inf, logits)
    v2 = jnp.max(rest, axis=-1, keepdims=True)
    i2 = jnp.min(jnp.where(rest == v2, col, big), axis=-1, keepdims=True)
    e = jnp.exp(v2 - v1)
    g1 = 1.0 / (1.0 + e)
    g2 = e / (1.0 + e)
    o_ref[...] = jnp.where(col == 0, i1.astype(F32),
                           jnp.where(col == 1, i2.astype(F32),
                                     jnp.where(col == 2, g1, jnp.where(col == 3, g2, 0.0))))


def _router(xn, wr_pad, n_exp, tm):
    m, d = xn.shape
    return pl.pallas_call(
        functools.partial(_router_body, n_exp),
        grid=(m // tm,),
        in_specs=[pl.BlockSpec((tm, d), lambda i: (i, 0)),
                  pl.BlockSpec(wr_pad.shape, lambda i: (0, 0))],
        out_specs=pl.BlockSpec((tm, LANES), lambda i: (i, 0)),
        out_shape=jax.ShapeDtypeStruct((m, LANES), F32),
        compiler_params=_cparams(("parallel",)),
        name="router",
    )(xn, wr_pad)


def _moe_body(be_ref, nv_ref, xs_ref, gate_ref, wg_ref, wu_ref, wd_ref, o_ref, acc_sc):
    i = pl.program_id(0)
    f = pl.program_id(1)
    last = pl.num_programs(1) - 1
    live = i < nv_ref[0]

    @pl.when(live & (f == 0))
    def _():
        acc_sc[...] = jnp.zeros(acc_sc.shape, F32)

    @pl.when(live)
    def _():
        x = xs_ref[...]
        g = jnp.dot(x, wg_ref[0], preferred_element_type=F32)
        u = jnp.dot(x, wu_ref[0], preferred_element_type=F32)
        hid = (g * jax.nn.sigmoid(g) * u).astype(BF16)
        acc_sc[...] += jnp.dot(hid, wd_ref[0], preferred_element_type=F32)

    @pl.when(live & (f == last))
    def _():
        o_ref[...] = acc_sc[...] * gate_ref[...]

    @pl.when(jnp.logical_not(live) & (f == last))
    def _():
        o_ref[...] = jnp.zeros(o_ref.shape, F32)


def _moe_experts(xs, gates, block_expert, n_valid, wg, wu, wd, tm, tf):
    p, d = xs.shape
    ff = wg.shape[2]
    grid_spec = pltpu.PrefetchScalarGridSpec(
        num_scalar_prefetch=2,
        grid=(p // tm, ff // tf),
        in_specs=[pl.BlockSpec((tm, d), lambda i, f, be, nv: (i, 0)),
                  pl.BlockSpec((tm, 1), lambda i, f, be, nv: (i, 0)),
                  pl.BlockSpec((1, d, tf), lambda i, f, be, nv: (be[i], 0, f)),
                  pl.BlockSpec((1, d, tf), lambda i, f, be, nv: (be[i], 0, f)),
                  pl.BlockSpec((1, tf, d), lambda i, f, be, nv: (be[i], f, 0))],
        out_specs=pl.BlockSpec((tm, d), lambda i, f, be, nv: (i, 0)),
        scratch_shapes=[pltpu.VMEM((tm, d), F32)],
    )
    return pl.pallas_call(
        _moe_body,
        grid_spec=grid_spec,
        out_shape=jax.ShapeDtypeStruct((p, d), F32),
        compiler_params=_cparams(("parallel", "arbitrary")),
        name="moe_experts",
    )(block_expert, n_valid, xs, gates, wg, wu, wd)


def _final_body(h_ref, y_ref, g_ref, o_ref):
    o_ref[...] = _rmsnorm_rows(h_ref[...] + y_ref[...], g_ref[...], RMS_EPS)


def _final_norm(h, y, g, tm):
    m, d = h.shape
    row = lambda i: (i, 0)
    return pl.pallas_call(
        _final_body,
        grid=(m // tm,),
        in_specs=[pl.BlockSpec((tm, d), row), pl.BlockSpec((tm, d), row),
                  pl.BlockSpec((1, d), lambda i: (0, 0))],
        out_specs=pl.BlockSpec((tm, d), row),
        out_shape=jax.ShapeDtypeStruct((m, d), F32),
        compiler_params=_cparams(("parallel",)),
        name="final_norm",
    )(h, y, g)


def _moe(xn_all, w_router, wg, wu, wd, tm, tf):
    n, d = xn_all.shape
    n_exp = w_router.shape[1]
    wr_pad = jnp.zeros((d, LANES), F32).at[:, :n_exp].set(w_router)
    route = _router(xn_all, wr_pad, n_exp, LANES)
    expert = route[:, :TOP_K].astype(jnp.int32).reshape(-1)
    gate = route[:, TOP_K:2 * TOP_K].reshape(-1)
    order = jnp.argsort(expert, stable=True)
    counts = jnp.bincount(expert, length=n_exp)
    padded = ((counts + tm - 1) // tm) * tm
    group_start = jnp.cumsum(padded) - padded
    sorted_start = jnp.cumsum(counts) - counts
    sorted_expert = expert[order]
    slot_sorted = group_start[sorted_expert] + (jnp.arange(n * TOP_K) - sorted_start[sorted_expert])
    n_slots = ((n * TOP_K + n_exp * (tm - 1)) // tm) * tm
    token_of_slot = jnp.zeros((n_slots,), jnp.int32).at[slot_sorted].set((order // TOP_K).astype(jnp.int32))
    gate_of_slot = jnp.zeros((n_slots,), F32).at[slot_sorted].set(gate[order])
    slot_of_assign = jnp.zeros((n * TOP_K,), jnp.int32).at[order].set(slot_sorted.astype(jnp.int32))
    n_blocks = n_slots // tm
    block_start = jnp.arange(n_blocks) * tm
    group_end = group_start + padded
    block_expert = jnp.minimum(jnp.sum(block_start[:, None] >= group_end[None, :], axis=1), n_exp - 1)
    n_valid = (jnp.sum(padded) // tm).astype(jnp.int32).reshape(1)
    last_valid = block_expert[jnp.maximum(n_valid[0] - 1, 0)]
    block_expert = jnp.where(jnp.arange(n_blocks) < n_valid[0], block_expert, last_valid).astype(jnp.int32)

    xs = jnp.take(xn_all.astype(BF16), token_of_slot, axis=0)
    ys = _moe_experts(xs, gate_of_slot[:, None], block_expert, n_valid, wg, wu, wd, tm, tf)
    picks = slot_of_assign.reshape(n, TOP_K)
    return jnp.take(ys, picks[:, 0], axis=0) + jnp.take(ys, picks[:, 1], axis=0)


def _pick(total, pref):
    t = min(total, pref)
    while total % t:
        t //= 2
    return t


def kernel(x_prompt, x_sample, cache_k, cache_v, page_table, w_in, w_out, norm_mix_g, norm_ffn_g,
           lam_q1, lam_k1, lam_q2, lam_k2, subln_g, gm_ln_g, gm_ln_b, gm_ws, gm_bs,
           ffn_wg, ffn_wu, ffn_wd, moe_router, moe_wg, moe_wu, moe_wd, final_norm_g):
    b, s, d = x_prompt.shape
    db, t = x_sample.shape[:2]
    depth = w_in.shape[0]
    n_pages = page_table.shape[1]
    past = n_pages * PAGE_SIZE
    n_p, n_s = b * s, db * t
    assert depth == 2 and s % CHUNK == 0

    tm_p = _pick(s, 512)
    blk = _pick(s, 256)
    pages_per_step = _pick(n_pages, 8)

    hp = x_prompt.reshape(n_p, d)
    hs = x_sample.transpose(1, 0, 2).reshape(n_s, d)
    tab_p = _rope_tables(jnp.arange(s, dtype=F32))
    pos_s = jnp.arange(t, dtype=F32) + jnp.float32(past)
    tab_s = _rope_tables(jnp.repeat(pos_s, db))
    ck = cache_k.reshape(cache_k.shape[0], cache_k.shape[1], PAGE_SIZE * DA_HEADS, LANES)
    cv = cache_v.reshape(cache_v.shape[0], cache_v.shape[1], PAGE_SIZE * DA_HEADS, LANES)
    lane = jnp.arange(LANES)
    comp_mask = jnp.stack([lane < DA_HEAD_DIM, lane >= DA_HEAD_DIM])
    new_rows = PAGE_SIZE
    assert t * DA_HEADS <= new_rows

    outs = {k: [] for k in ("kp", "vp", "ks", "vs", "gv")}
    y_p = y_s = None
    for l in range(depth):
        lam_init = 0.8 - 0.6 * math.exp(-0.3 * l)
        w_in_bf = w_in[l].astype(BF16)
        w_out_bf = w_out[l].astype(BF16)
        g_mix = norm_mix_g[l][None]
        g_ffn = norm_ffn_g[l][None]
        lng, lnb = gm_ln_g[l][None], gm_ln_b[l][None]
        lams = (lam_q1[l][None], lam_k1[l][None], lam_q2[l][None], lam_k2[l][None])
        sg = subln_g[l][None]

        q, k, v, kb, vb, u, gv = _proj(hp, g_mix, w_in_bf, tab_p, lng, lnb, tm_p, BF16, BF16)
        a = _attn_prompt(q.reshape(b, s, QK_COLS), kb.reshape(b, s, QK_COLS), vb.reshape(b, s, DA_WIDTH),
                         lams, sg, lam_init, blk)
        hp, xn_p = _mix_out(a.reshape(n_p, DA_WIDTH), u, gv, hp, gm_ws[l], gm_bs[l], w_out_bf, g_ffn, tm_p)

        sq, sk, sv, skb, svb, su, sgv = _proj(hs, g_mix, w_in_bf, tab_s, lng, lnb, n_s, F32, F32)
        q5 = sq.reshape(t, db, DA_HEADS, 1, LANES).transpose(1, 2, 3, 0, 4)
        q_rows = jnp.where(comp_mask[None, None, :, None, :], q5, jnp.zeros((), BF16))
        q_rows = q_rows.reshape(db, DA_HEADS * 2 * t, LANES)
        pad = ((0, 0), (0, new_rows - t * DA_HEADS), (0, 0))
        kn = jnp.pad(skb.reshape(t, db, DA_HEADS, LANES).transpose(1, 0, 2, 3).reshape(db, t * DA_HEADS, LANES), pad)
        vn = jnp.pad(svb.reshape(t, db, DA_HEADS, LANES).transpose(1, 0, 2, 3).reshape(db, t * DA_HEADS, LANES), pad)
        sa = _attn_sample(l, q_rows, kn, vn, ck, cv, page_table, lams, sg, lam_init, t, pages_per_step)
        sa = sa.transpose(1, 0, 2).reshape(n_s, DA_WIDTH).astype(BF16)
        hs, xn_s = _mix_out(sa, su, sgv, hs, gm_ws[l], gm_bs[l], w_out_bf, g_ffn, n_s, sample_shape=(t, db))

        outs["kp"].append(k.reshape(b, s, DA_HEADS, 2 * DA_HEAD_DIM))
        outs["vp"].append(v.reshape(b, s, DA_HEADS, DA_V_DIM))
        outs["ks"].append(sk.reshape(t, db, DA_HEADS, 2 * DA_HEAD_DIM).transpose(1, 0, 2, 3))
        outs["vs"].append(sv.reshape(t, db, DA_HEADS, DA_V_DIM).transpose(1, 0, 2, 3))
        outs["gv"].append(sgv.reshape(t, db, GM_GROUPS, GM_CH).transpose(1, 0, 2, 3))

        j = l // 2
        if l % 2 == 0:
            wg, wu, wd = ffn_wg[j].astype(BF16), ffn_wu[j].astype(BF16), ffn_wd[j].astype(BF16)
            tf = _pick(wg.shape[1], 1408)
            hp = _ffn(xn_p, hp, wg, wu, wd, tm_p, tf)
            hs = _ffn(xn_s, hs, wg, wu, wd, n_s, tf)
        else:
            wg, wu, wd = moe_wg[j].astype(BF16), moe_wu[j].astype(BF16), moe_wd[j].astype(BF16)
            y = _moe(jnp.concatenate([xn_p, xn_s], axis=0), moe_router[j], wg, wu, wd,
                     512, _pick(wg.shape[2], 512))
            y_p, y_s = y[:n_p], y[n_p:]

    fg = final_norm_g[None]
    y_prompt = _final_norm(hp, y_p, fg, tm_p).reshape(b, s, d)
    y_sample = _final_norm(hs, y_s, fg, n_s).reshape(t, db, d).transpose(1, 0, 2)
    return (y_prompt, y_sample, jnp.stack(outs["kp"]), jnp.stack(outs["vp"]),
            jnp.stack(outs["ks"]), jnp.stack(outs["vs"]), jnp.stack(outs["gv"]))
```

```python
import functools
import math

import jax
import jax.numpy as jnp
from jax import lax
from jax.experimental import pallas as pl
from jax.experimental.pallas import tpu as pltpu

F32 = jnp.float32
BF16 = jnp.bfloat16

DA_HEADS = 4
DA_HEAD_DIM = 64
DA_V_DIM = 2 * DA_HEAD_DIM
QK_COLS = DA_HEADS * 2 * DA_HEAD_DIM
DA_WIDTH = DA_HEADS * DA_V_DIM
ROT_DIM = DA_HEAD_DIM // 4
ROPE_THETA = 500000.0
GM_GROUPS = 4
GM_CH = 128
GM_WIDTH = GM_GROUPS * GM_CH
CHUNK = 128
PAGE_SIZE = 128
TOP_K = 2
RMS_EPS = 1e-6
SUBLN_EPS = 1e-5
LN_EPS = 1e-5
NEG_BIG = -1e30
LOG2_E = math.log2(math.e)
LANES = 128

VMEM_LIMIT = 52 * 1024 * 1024


def _cparams(sem):
    return pltpu.CompilerParams(dimension_semantics=sem, vmem_limit_bytes=VMEM_LIMIT)


def _nt_dot(a, b):
    return lax.dot_general(a, b, (((1,), (1,)), ((), ())), preferred_element_type=F32)


def _rmsnorm_rows(x, g, eps):
    return x * lax.rsqrt(jnp.mean(x * x, axis=-1, keepdims=True) + eps) * g


def _gelu_exact(x):
    return 0.5 * x * (1.0 + lax.erf(x * (2.0 ** -0.5)))


def _diff_lambda(lq1, lk1, lq2, lk2, lam_init):
    a = jnp.exp(jnp.sum(lq1 * lk1, axis=-1, keepdims=True))
    b = jnp.exp(jnp.sum(lq2 * lk2, axis=-1, keepdims=True))
    return a - b + lam_init


def _proj_body(h_ref, g_ref, w_ref, cos_ref, sa_ref, sb_ref, lng_ref, lnb_ref,
               q_ref, k_ref, v_ref, kb_ref, vb_ref, u_ref, gv_ref):
    h = h_ref[...]
    xn = _rmsnorm_rows(h, g_ref[...], RMS_EPS).astype(BF16)
    z = jnp.dot(xn, w_ref[...], preferred_element_type=F32)
    cos, sa, sb = cos_ref[...], sa_ref[...], sb_ref[...]

    def rope(x):
        return (x * cos + pltpu.roll(x, ROT_DIM // 2, 1) * sa
                + pltpu.roll(x, LANES - ROT_DIM // 2, 1) * sb)

    scale = DA_HEAD_DIM ** -0.5 * LOG2_E
    ones = jnp.ones((h.shape[0], DA_V_DIM), vb_ref.dtype)
    for t in range(DA_HEADS):
        sl = slice(t * LANES, (t + 1) * LANES)
        q_ref[:, sl] = (rope(z[:, sl]) * scale).astype(q_ref.dtype)
        kt = rope(z[:, QK_COLS + t * LANES:QK_COLS + (t + 1) * LANES])
        k_ref[:, sl] = kt
        kb_ref[:, sl] = kt.astype(kb_ref.dtype)
        vt = z[:, 2 * QK_COLS + t * DA_V_DIM:2 * QK_COLS + (t + 1) * DA_V_DIM]
        v_ref[:, sl] = vt
        vb_ref[:, 2 * t * DA_V_DIM:(2 * t + 1) * DA_V_DIM] = vt.astype(vb_ref.dtype)
        vb_ref[:, (2 * t + 1) * DA_V_DIM:(2 * t + 2) * DA_V_DIM] = ones
    gz = _gelu_exact(z[:, 2 * QK_COLS + DA_WIDTH:])
    u_ref[...] = gz[:, :GM_WIDTH].astype(u_ref.dtype)
    gvr = gz[:, GM_WIDTH:]
    mu = jnp.mean(gvr, axis=-1, keepdims=True)
    d = gvr - mu
    var = jnp.mean(d * d, axis=-1, keepdims=True)
    gv = d * lax.rsqrt(var + LN_EPS) * lng_ref[...] + lnb_ref[...]
    gv_ref[...] = gv.astype(gv_ref.dtype)


def _proj(h, g, w_bf, tables, lng, lnb, tm, u_dtype, gv_dtype):
    m, d = h.shape
    cos_t, sa_t, sb_t = tables
    ntab = cos_t.shape[0] // tm
    row = lambda i: (i, 0)
    fixed = lambda i: (0, 0)
    tab = lambda i: (i % ntab, 0)
    wide = lambda dt: jax.ShapeDtypeStruct((m, QK_COLS), dt)
    return pl.pallas_call(
        _proj_body,
        grid=(m // tm,),
        in_specs=[
            pl.BlockSpec((tm, d), row),
            pl.BlockSpec((1, d), fixed),
            pl.BlockSpec(w_bf.shape, fixed),
            pl.BlockSpec((tm, LANES), tab),
            pl.BlockSpec((tm, LANES), tab),
            pl.BlockSpec((tm, LANES), tab),
            pl.BlockSpec((1, GM_WIDTH), fixed),
            pl.BlockSpec((1, GM_WIDTH), fixed),
        ],
        out_specs=[pl.BlockSpec((tm, QK_COLS), row)] * 4
        + [pl.BlockSpec((tm, 2 * DA_WIDTH), row)] + [pl.BlockSpec((tm, GM_WIDTH), row)] * 2,
        out_shape=[wide(BF16), wide(F32), wide(F32), wide(BF16),
                   jax.ShapeDtypeStruct((m, 2 * DA_WIDTH), BF16), wide(u_dtype), wide(gv_dtype)],
        compiler_params=_cparams(("parallel",)),
        name="proj",
    )(h, g, w_bf, cos_t, sa_t, sb_t, lng, lnb)


def _rope_tables(pos):
    inv = ROPE_THETA ** (-jnp.arange(0, ROT_DIM, 2, dtype=F32) / ROT_DIM)
    ang = pos[:, None] * inv[None, :]
    c, s = jnp.cos(ang), jnp.sin(ang)
    j = jnp.arange(LANES) % DA_HEAD_DIM
    first = j < ROT_DIM // 2
    second = (j >= ROT_DIM // 2) & (j < ROT_DIM)
    idx = j % (ROT_DIM // 2)
    cg, sg = c[:, idx], s[:, idx]
    cos_t = jnp.where(first | second, cg, 1.0)
    sa_t = jnp.where(second, sg, 0.0)
    sb_t = jnp.where(first, -sg, 0.0)
    return cos_t, sa_t, sb_t


def _attn_prompt_body(lam_init, blk, lq1_ref, lk1_ref, lq2_ref, lk2_ref, sg_ref,
                      q_ref, k_ref, v_ref, o_ref, s_sc, mx_sc, acc_sc):
    n_q = q_ref.shape[1] // blk
    lane = lax.broadcasted_iota(jnp.int32, (blk, LANES), 1)
    r2 = lax.broadcasted_iota(jnp.int32, (2 * blk, blk), 0)
    c2 = lax.broadcasted_iota(jnp.int32, (2 * blk, blk), 1)
    causal = c2 <= jnp.where(r2 >= blk, r2 - blk, r2)
    lam = _diff_lambda(lq1_ref[...], lk1_ref[...], lq2_ref[...], lk2_ref[...], lam_init)
    gain = sg_ref[...] * (1.0 - lam_init)

    def fold(s):
        out = s[:, :LANES]
        for i in range(1, blk // LANES):
            out = jnp.maximum(out, s[:, i * LANES:(i + 1) * LANES])
        return out

    def q_block(qi, carry):
        q0 = pl.multiple_of(qi * blk, blk)
        q = q_ref[0, pl.ds(q0, blk), :]
        zero = jnp.zeros_like(q)
        q_st = jnp.concatenate([jnp.where(lane < DA_HEAD_DIM, q, zero),
                                jnp.where(lane >= DA_HEAD_DIM, q, zero)], axis=0)
        mx_sc[...] = jnp.full(mx_sc.shape, NEG_BIG, F32)

        def sweep1(j, c):
            k0 = pl.multiple_of(j * blk, blk)
            s = _nt_dot(q_st, k_ref[0, pl.ds(k0, blk), :])
            s_sc[j] = s
            mx_sc[...] = jnp.maximum(mx_sc[...], fold(s))
            return c

        lax.fori_loop(0, qi, sweep1, 0)
        s = jnp.where(causal, _nt_dot(q_st, k_ref[0, pl.ds(q0, blk), :]), NEG_BIG)
        s_sc[qi] = s
        m = jnp.max(jnp.maximum(mx_sc[...], fold(s)), axis=1, keepdims=True)
        mx_sc[...] = jnp.broadcast_to(m, mx_sc.shape)
        acc_sc[...] = jnp.zeros(acc_sc.shape, F32)

        def sweep2(j, c):
            k0 = pl.multiple_of(j * blk, blk)
            mb = mx_sc[...]
            p = jnp.exp2(s_sc[j] - jnp.concatenate([mb] * (blk // LANES), axis=1)).astype(BF16)
            acc_sc[...] += jnp.dot(p, v_ref[0, pl.ds(k0, blk), :], preferred_element_type=F32)
            return c

        lax.fori_loop(0, qi + 1, sweep2, 0)
        acc = acc_sc[...]
        o = (acc[:blk, :DA_V_DIM] / acc[:blk, DA_V_DIM:]
             - lam * (acc[blk:, :DA_V_DIM] / acc[blk:, DA_V_DIM:]))
        o = _rmsnorm_rows(o, gain, SUBLN_EPS)
        o_ref[0, pl.ds(q0, blk), :] = o.astype(o_ref.dtype)
        return carry

    lax.fori_loop(0, n_q, q_block, 0)


def _attn_prompt(q, kb, vb1, lams, sg, lam_init, blk):
    b, s, _ = q.shape
    small = pl.BlockSpec((1, DA_HEAD_DIM), lambda bi, h: (0, 0))
    per_head = pl.BlockSpec((1, s, LANES), lambda bi, h: (bi, 0, h))
    return pl.pallas_call(
        functools.partial(_attn_prompt_body, lam_init, blk),
        grid=(b, DA_HEADS),
        in_specs=[small, small, small, small,
                  pl.BlockSpec((1, DA_V_DIM), lambda bi, h: (0, 0)),
                  per_head, per_head,
                  pl.BlockSpec((1, s, 2 * DA_V_DIM), lambda bi, h: (bi, 0, h))],
        out_specs=per_head,
        out_shape=jax.ShapeDtypeStruct((b, s, DA_WIDTH), BF16),
        scratch_shapes=[pltpu.VMEM((s // blk, 2 * blk, blk), F32),
                        pltpu.VMEM((2 * blk, LANES), F32),
                        pltpu.VMEM((2 * blk, 2 * DA_V_DIM), F32)],
        compiler_params=_cparams(("parallel", "parallel")),
        name="attn_prompt",
    )(*lams, sg, q, kb, vb1)


def _attn_sample_body(lam_init, n_tok, pages_per_step, pt_ref, lq1_ref, lk1_ref, lq2_ref, lk2_ref,
                      sg_ref, q_ref, kn_ref, vn_ref, *rest):
    kp = rest[:pages_per_step]
    vp = rest[pages_per_step:2 * pages_per_step]
    o_ref = rest[2 * pages_per_step]
    m_sc, l_sc, a_sc = rest[2 * pages_per_step + 1:]
    j = pl.program_id(1)
    nrow = DA_HEADS * 2 * n_tok
    ncol = PAGE_SIZE * DA_HEADS

    @pl.when(j == 0)
    def _():
        m_sc[...] = jnp.full(m_sc.shape, NEG_BIG, F32)
        l_sc[...] = jnp.zeros(l_sc.shape, F32)
        a_sc[...] = jnp.zeros(a_sc.shape, F32)

    q = q_ref[0]
    row = lax.broadcasted_iota(jnp.int32, (nrow, ncol), 0)
    colk = lax.broadcasted_iota(jnp.int32, (nrow, ncol), 1)
    same_head = (colk % DA_HEADS) == (row // (2 * n_tok))

    def update(s_list, v_list):
        m = m_sc[...]
        mn = m
        for s in s_list:
            mn = jnp.maximum(mn, jnp.max(s, axis=-1, keepdims=True))
        alpha = jnp.exp2(m - mn)
        l = alpha * l_sc[...]
        a = alpha * a_sc[...]
        for s, v in zip(s_list, v_list):
            p = jnp.exp2(s - mn)
            l = l + jnp.sum(p, axis=-1, keepdims=True)
            a = a + jnp.dot(p.astype(BF16), v, preferred_element_type=F32)
        m_sc[...] = mn
        l_sc[...] = l
        a_sc[...] = a

    s_list, v_list = [], []
    for i in range(pages_per_step):
        kb = kp[i][0, 0].astype(BF16)
        s_list.append(jnp.where(same_head, _nt_dot(q, kb), NEG_BIG))
        v_list.append(vp[i][0, 0].astype(BF16))
    update(s_list, v_list)

    @pl.when(j == pl.num_programs(1) - 1)
    def _():
        rown = lax.broadcasted_iota(jnp.int32, (nrow, kn_ref.shape[1]), 0)
        coln = lax.broadcasted_iota(jnp.int32, (nrow, kn_ref.shape[1]), 1)
        ok = ((coln < n_tok * DA_HEADS) & ((coln % DA_HEADS) == (rown // (2 * n_tok)))
              & ((coln // DA_HEADS) <= (rown % n_tok)))
        s = jnp.where(ok, _nt_dot(q, kn_ref[0]), NEG_BIG)
        update([s], [vn_ref[0]])
        a = a_sc[...] / l_sc[...]
        lam = _diff_lambda(lq1_ref[...], lk1_ref[...], lq2_ref[...], lk2_ref[...], lam_init)
        for h in range(DA_HEADS):
            base = h * 2 * n_tok
            o = a[base:base + n_tok] - lam * a[base + n_tok:base + 2 * n_tok]
            o = _rmsnorm_rows(o, sg_ref[...], SUBLN_EPS) * (1.0 - lam_init)
            o_ref[0, :, h * LANES:(h + 1) * LANES] = o


def _attn_sample(layer, q_rows, kn, vn, cache_k, cache_v, page_table, lams, sg, lam_init,
                 n_tok, pages_per_step):
    db, nrow, _ = q_rows.shape
    n_pages = page_table.shape[1]
    assert n_pages % pages_per_step == 0
    pt = page_table.reshape(-1)
    small = pl.BlockSpec((1, DA_HEAD_DIM), lambda b, j, p: (0, 0))
    per_seq = lambda b, j, p: (b, 0, 0)

    def page_spec(i):
        return pl.BlockSpec(
            (1, 1, PAGE_SIZE * DA_HEADS, LANES),
            lambda b, j, p: (layer, p[b * n_pages + j * pages_per_step + i], 0, 0))

    pages = [page_spec(i) for i in range(pages_per_step)]
    grid_spec = pltpu.PrefetchScalarGridSpec(
        num_scalar_prefetch=1,
        grid=(db, n_pages // pages_per_step),
        in_specs=[small, small, small, small,
                  pl.BlockSpec((1, DA_V_DIM), lambda b, j, p: (0, 0)),
                  pl.BlockSpec((1, nrow, LANES), per_seq),
                  pl.BlockSpec((1,) + kn.shape[1:], per_seq),
                  pl.BlockSpec((1,) + vn.shape[1:], per_seq)] + pages + pages,
        out_specs=pl.BlockSpec((1, n_tok, DA_WIDTH), per_seq),
        scratch_shapes=[pltpu.VMEM((nrow, 1), F32), pltpu.VMEM((nrow, 1), F32),
                        pltpu.VMEM((nrow, DA_V_DIM), F32)],
    )
    return pl.pallas_call(
        functools.partial(_attn_sample_body, lam_init, n_tok, pages_per_step),
        grid_spec=grid_spec,
        out_shape=jax.ShapeDtypeStruct((db, n_tok, DA_WIDTH), F32),
        compiler_params=_cparams(("parallel", "arbitrary")),
        name="attn_sample",
    )(pt, *lams, sg, q_rows, kn, vn, *([cache_k] * pages_per_step), *([cache_v] * pages_per_step))


def _mix_out_prompt_body(n_chunk, a_ref, u_ref, gv_ref, h_ref, ws_ref, bst_ref, wo_ref, gf_ref,
                         ho_ref, xn_ref, g_sc):
    r = lax.broadcasted_iota(jnp.int32, (CHUNK, CHUNK), 0)
    c = lax.broadcasted_iota(jnp.int32, (CHUNK, CHUNK), 1)
    for g in range(GM_GROUPS):
        w = jnp.where(c <= r, ws_ref[g], 0.0).astype(BF16)
        bias = bst_ref[:, g:g + 1]
        lanes = slice(g * GM_CH, (g + 1) * GM_CH)
        for ci in range(n_chunk):
            rows = slice(ci * CHUNK, (ci + 1) * CHUNK)
            m = jnp.dot(w, gv_ref[rows, lanes], preferred_element_type=F32) + bias
            g_sc[rows, lanes] = (u_ref[rows, lanes].astype(F32) * m).astype(BF16)
    out = (jnp.dot(a_ref[...], wo_ref[:DA_WIDTH, :], preferred_element_type=F32)
           + jnp.dot(g_sc[...], wo_ref[DA_WIDTH:, :], preferred_element_type=F32))
    hn = h_ref[...] + out
    ho_ref[...] = hn
    xn_ref[...] = _rmsnorm_rows(hn, gf_ref[...], RMS_EPS)


def _mix_out_sample_body(n_tok, n_seq, a_ref, u_ref, gv_ref, h_ref, ws_ref, bs_ref, wo_ref, gf_ref,
                         ho_ref, xn_ref, g_sc):
    for g in range(GM_GROUPS):
        lanes = slice(g * GM_CH, (g + 1) * GM_CH)
        for t in range(n_tok):
            m = jnp.zeros((n_seq, GM_CH), F32) + bs_ref[g:g + 1, t:t + 1]
            for s in range(t + 1):
                m = m + ws_ref[g, t:t + 1, s:s + 1] * gv_ref[s * n_seq:(s + 1) * n_seq, lanes]
            rows = slice(t * n_seq, (t + 1) * n_seq)
            g_sc[rows, lanes] = u_ref[rows, lanes] * m
    out = (jnp.dot(a_ref[...], wo_ref[:DA_WIDTH, :], preferred_element_type=F32)
           + jnp.dot(g_sc[...].astype(BF16), wo_ref[DA_WIDTH:, :], preferred_element_type=F32))
    hn = h_ref[...] + out
    ho_ref[...] = hn
    xn_ref[...] = _rmsnorm_rows(hn, gf_ref[...], RMS_EPS)


def _mix_out(a, u, gv, h, ws, bs, wo_bf, gf, tm, sample_shape=None):
    m, d = h.shape
    row = lambda i: (i, 0)
    fixed2 = lambda i: (0, 0)
    fixed3 = lambda i: (0, 0, 0)
    if sample_shape is None:
        body = functools.partial(_mix_out_prompt_body, tm // CHUNK)
        bias = bs.T
        g_dtype = BF16
    else:
        body = functools.partial(_mix_out_sample_body, *sample_shape)
        bias = bs
        g_dtype = F32
    return pl.pallas_call(
        body,
        grid=(m // tm,),
        in_specs=[pl.BlockSpec((tm, DA_WIDTH), row),
                  pl.BlockSpec((tm, GM_WIDTH), row),
                  pl.BlockSpec((tm, GM_WIDTH), row),
                  pl.BlockSpec((tm, d), row),
                  pl.BlockSpec(ws.shape, fixed3),
                  pl.BlockSpec(bias.shape, fixed2),
                  pl.BlockSpec(wo_bf.shape, fixed2),
                  pl.BlockSpec((1, d), fixed2)],
        out_specs=[pl.BlockSpec((tm, d), row), pl.BlockSpec((tm, d), row)],
        out_shape=[jax.ShapeDtypeStruct((m, d), F32), jax.ShapeDtypeStruct((m, d), F32)],
        scratch_shapes=[pltpu.VMEM((tm, GM_WIDTH), g_dtype)],
        compiler_params=_cparams(("parallel",)),
        name="mix_out",
    )(a, u, gv, h, ws, bias, wo_bf, gf)


def _ffn_body(xn_ref, h_ref, wg_ref, wu_ref, wd_ref, o_ref, acc_sc):
    f = pl.program_id(1)

    @pl.when(f == 0)
    def _():
        acc_sc[...] = jnp.zeros(acc_sc.shape, F32)

    x = xn_ref[...].astype(BF16)
    g = jnp.dot(x, wg_ref[...], preferred_element_type=F32)
    u = jnp.dot(x, wu_ref[...], preferred_element_type=F32)
    hid = (g * jax.nn.sigmoid(g) * u).astype(BF16)
    acc_sc[...] += jnp.dot(hid, wd_ref[...], preferred_element_type=F32)

    @pl.when(f == pl.num_programs(1) - 1)
    def _():
        o_ref[...] = h_ref[...] + acc_sc[...]


def _ffn(xn, h, wg, wu, wd, tm, tf):
    m, d = h.shape
    ff = wg.shape[1]
    return pl.pallas_call(
        _ffn_body,
        grid=(m // tm, ff // tf),
        in_specs=[pl.BlockSpec((tm, d), lambda i, f: (i, 0)),
                  pl.BlockSpec((tm, d), lambda i, f: (i, 0)),
                  pl.BlockSpec((d, tf), lambda i, f: (0, f)),
                  pl.BlockSpec((d, tf), lambda i, f: (0, f)),
                  pl.BlockSpec((tf, d), lambda i, f: (f, 0))],
        out_specs=pl.BlockSpec((tm, d), lambda i, f: (i, 0)),
        out_shape=jax.ShapeDtypeStruct((m, d), F32),
        scratch_shapes=[pltpu.VMEM((tm, d), F32)],
        compiler_params=_cparams(("parallel", "arbitrary")),
        name="ffn",
    )(xn, h, wg, wu, wd)


R_E1, R_E2, R_G1, R_G2, R_RANK1, R_RANK2 = range(6)


def _router_body(n_exp, xn_ref, wr_ref, base_ref, o_ref, cnt_ref, cnt_sc):
    i = pl.program_id(0)

    @pl.when(i == 0)
    def _():
        cnt_sc[...] = base_ref[...]

    logits = jnp.dot(xn_ref[...], wr_ref[...], preferred_element_type=F32,
                     precision=lax.Precision.HIGHEST)
    tm = logits.shape[0]
    col = lax.broadcasted_iota(jnp.int32, logits.shape, 1)
    logits = jnp.where(col < n_exp, logits, -jnp.inf)
    big = jnp.int32(LANES)
    v1 = jnp.max(logits, axis=-1, keepdims=True)
    i1 = jnp.min(jnp.where(logits == v1, col, big), axis=-1, keepdims=True)
    rest = jnp.where(col == i1, -jnp.inf, logits)
    v2 = jnp.max(rest, axis=-1, keepdims=True)
    i2 = jnp.min(jnp.where(rest == v2, col, big), axis=-1, keepdims=True)
    e = jnp.exp(v2 - v1)
    g1 = 1.0 / (1.0 + e)
    g2 = e / (1.0 + e)
    hit = (col == i1) | (col == i2)
    r = lax.broadcasted_iota(jnp.int32, (tm, tm), 0)
    c = lax.broadcasted_iota(jnp.int32, (tm, tm), 1)
    below = jnp.where(c < r, 1.0, 0.0).astype(BF16)
    prefix = jnp.dot(below, jnp.where(hit, 1.0, 0.0).astype(BF16), preferred_element_type=F32)
    rank = prefix + cnt_sc[...]
    r1 = jnp.sum(jnp.where(col == i1, rank, 0.0), axis=-1, keepdims=True)
    r2 = jnp.sum(jnp.where(col == i2, rank, 0.0), axis=-1, keepdims=True)
    cnt_sc[...] += jnp.sum(jnp.where(hit, 1.0, 0.0), axis=0, keepdims=True)
    rec = jnp.zeros(logits.shape, F32)
    for idx, val in ((R_E1, i1.astype(F32)), (R_E2, i2.astype(F32)), (R_G1, g1), (R_G2, g2),
                     (R_RANK1, r1), (R_RANK2, r2)):
        rec = jnp.where(col == idx, val, rec)
    o_ref[...] = rec
    cnt_ref[...] = cnt_sc[...]


def _router(xn, wr_pad, base_counts, n_exp, tm):
    m, d = xn.shape
    return pl.pallas_call(
        functools.partial(_router_body, n_exp),
        grid=(m // tm,),
        in_specs=[pl.BlockSpec((tm, d), lambda i: (i, 0)),
                  pl.BlockSpec(wr_pad.shape, lambda i: (0, 0)),
                  pl.BlockSpec((1, LANES), lambda i: (0, 0))],
        out_specs=[pl.BlockSpec((tm, LANES), lambda i: (i, 0)),
                   pl.BlockSpec((1, LANES), lambda i: (0, 0))],
        out_shape=[jax.ShapeDtypeStruct((m, LANES), F32), jax.ShapeDtypeStruct((1, LANES), F32)],
        scratch_shapes=[pltpu.VMEM((1, LANES), F32)],
        compiler_params=_cparams(("arbitrary",)),
        name="router",
    )(xn, wr_pad, base_counts)


def _row_copy(src, src_row, dst, dst_row, sem):
    return pltpu.make_async_copy(src.at[pl.ds(src_row, 1), :], dst.at[pl.ds(dst_row, 1), :], sem)


def _dispatch_body(slot_ref, xn_ref, xs_in_ref, xs_ref, sem):
    del xs_in_ref
    tm = xn_ref.shape[0]

    def start(t, c):
        for k in range(TOP_K):
            _row_copy(xn_ref, t, xs_ref, slot_ref[0, 0, TOP_K * t + k], sem).start()
        return c

    def wait(t, c):
        for k in range(TOP_K):
            _row_copy(xn_ref, t, xs_ref, slot_ref[0, 0, TOP_K * t + k], sem).wait()
        return c

    lax.fori_loop(0, tm, start, 0)
    lax.fori_loop(0, tm, wait, 0)


def _dispatch(slots, xn, xs, tm):
    m, d = xn.shape
    slots3 = slots.reshape(m // tm, 1, tm * TOP_K)
    return pl.pallas_call(
        _dispatch_body,
        grid=(m // tm,),
        in_specs=[pl.BlockSpec((1, 1, tm * TOP_K), lambda i: (i, 0, 0), memory_space=pltpu.SMEM),
                  pl.BlockSpec((tm, d), lambda i: (i, 0)),
                  pl.BlockSpec(memory_space=pl.ANY)],
        out_specs=pl.BlockSpec(memory_space=pl.ANY),
        out_shape=jax.ShapeDtypeStruct(xs.shape, xs.dtype),
        scratch_shapes=[pltpu.SemaphoreType.DMA(())],
        input_output_aliases={2: 0},
        compiler_params=_cparams(("arbitrary",)),
        name="moe_dispatch",
    )(slots3, xn, xs)


def _combine_body(slot_ref, rec_ref, h_ref, g_ref, ys_ref, o_ref, buf, sem):
    tm = h_ref.shape[0]

    def start(t, c):
        for k in range(TOP_K):
            _row_copy(ys_ref, slot_ref[0, 0, TOP_K * t + k], buf.at[k], t, sem).start()
        return c

    def wait(t, c):
        for k in range(TOP_K):
            _row_copy(ys_ref, slot_ref[0, 0, TOP_K * t + k], buf.at[k], t, sem).wait()
        return c

    lax.fori_loop(0, tm, start, 0)
    lax.fori_loop(0, tm, wait, 0)
    rec = rec_ref[...]
    y = rec[:, R_G1:R_G1 + 1] * buf[0] + rec[:, R_G2:R_G2 + 1] * buf[1]
    o_ref[...] = _rmsnorm_rows(h_ref[...] + y, g_ref[...], RMS_EPS)


def _combine_norm(slots, rec, h, g, ys, tm):
    m, d = h.shape
    slots3 = slots.reshape(m // tm, 1, tm * TOP_K)
    row = lambda i: (i, 0)
    return pl.pallas_call(
        _combine_body,
        grid=(m // tm,),
        in_specs=[pl.BlockSpec((1, 1, tm * TOP_K), lambda i: (i, 0, 0), memory_space=pltpu.SMEM),
                  pl.BlockSpec((tm, LANES), row),
                  pl.BlockSpec((tm, d), row),
                  pl.BlockSpec((1, d), lambda i: (0, 0)),
                  pl.BlockSpec(memory_space=pl.ANY)],
        out_specs=pl.BlockSpec((tm, d), row),
        out_shape=jax.ShapeDtypeStruct((m, d), F32),
        scratch_shapes=[pltpu.VMEM((TOP_K, tm, d), F32), pltpu.SemaphoreType.DMA(())],
        compiler_params=_cparams(("arbitrary",)),
        name="moe_combine_norm",
    )(slots3, rec, h, g, ys)


def _moe_body(be_ref, nv_ref, xs_ref, wg_ref, wu_ref, wd_ref, o_ref, acc_sc):
    i = pl.program_id(0)
    f = pl.program_id(1)
    last = pl.num_programs(1) - 1
    live = i < nv_ref[0]

    @pl.when(live & (f == 0))
    def _():
        acc_sc[...] = jnp.zeros(acc_sc.shape, F32)

    @pl.when(live)
    def _():
        x = xs_ref[...].astype(BF16)
        g = jnp.dot(x, wg_ref[0], preferred_element_type=F32)
        u = jnp.dot(x, wu_ref[0], preferred_element_type=F32)
        hid = (g * jax.nn.sigmoid(g) * u).astype(BF16)
        acc_sc[...] += jnp.dot(hid, wd_ref[0], preferred_element_type=F32)

    @pl.when(live & (f == last))
    def _():
        o_ref[...] = acc_sc[...]

    @pl.when(jnp.logical_not(live) & (f == last))
    def _():
        o_ref[...] = jnp.zeros(o_ref.shape, F32)


def _moe_experts(xs, block_expert, n_valid, wg, wu, wd, tm, tf):
    p, d = xs.shape
    ff = wg.shape[2]
    grid_spec = pltpu.PrefetchScalarGridSpec(
        num_scalar_prefetch=2,
        grid=(p // tm, ff // tf),
        in_specs=[pl.BlockSpec((tm, d), lambda i, f, be, nv: (i, 0)),
                  pl.BlockSpec((1, d, tf), lambda i, f, be, nv: (be[i], 0, f)),
                  pl.BlockSpec((1, d, tf), lambda i, f, be, nv: (be[i], 0, f)),
                  pl.BlockSpec((1, tf, d), lambda i, f, be, nv: (be[i], f, 0))],
        out_specs=pl.BlockSpec((tm, d), lambda i, f, be, nv: (i, 0)),
        scratch_shapes=[pltpu.VMEM((tm, d), F32)],
    )
    return pl.pallas_call(
        _moe_body,
        grid_spec=grid_spec,
        out_shape=jax.ShapeDtypeStruct((p, d), F32),
        compiler_params=_cparams(("parallel", "arbitrary")),
        name="moe_experts",
    )(block_expert, n_valid, xs, wg, wu, wd)


def _moe_final(groups, w_router, wg, wu, wd, final_g, tm, tf):
    d = w_router.shape[0]
    n_exp = w_router.shape[1]
    n_tok = sum(xn.shape[0] for xn, _, _ in groups)
    wr_pad = jnp.zeros((d, LANES), F32).at[:, :n_exp].set(w_router)
    counts = jnp.zeros((1, LANES), F32)
    recs = []
    for xn, _, _ in groups:
        rec, counts = _router(xn, wr_pad, counts, n_exp, _pick(xn.shape[0], LANES))
        recs.append(rec)
    cnt = counts[0, :n_exp].astype(jnp.int32)
    padded = ((cnt + tm - 1) // tm) * tm
    group_end = jnp.cumsum(padded)
    group_start = group_end - padded
    n_slots = ((n_tok * TOP_K + n_exp * (tm - 1)) // tm) * tm
    n_blocks = n_slots // tm
    block_start = jnp.arange(n_blocks, dtype=jnp.int32) * tm
    n_valid = (group_end[-1] // tm).astype(jnp.int32)
    block_expert = jnp.sum((block_start[:, None] >= group_end[None, :]).astype(jnp.int32), axis=1)
    last_valid = jnp.sum((jnp.maximum(n_valid - 1, 0) * tm >= group_end).astype(jnp.int32))
    block_expert = jnp.minimum(jnp.where(block_start < n_valid * tm, block_expert, last_valid),
                               n_exp - 1).astype(jnp.int32)

    def slots_of(rec):
        e = rec[:, R_E1:R_E2 + 1].astype(jnp.int32)
        rank = rec[:, R_RANK1:R_RANK2 + 1].astype(jnp.int32)
        start = jnp.sum(jnp.where(e[..., None] == jnp.arange(n_exp), group_start, 0), axis=-1)
        return (start + rank).astype(jnp.int32)

    slots = [slots_of(rec) for rec in recs]
    xs = jnp.zeros((n_slots, d), F32)
    for (xn, _, tb), sl in zip(groups, slots):
        xs = _dispatch(sl, xn, xs, tb)
    ys = _moe_experts(xs, block_expert, n_valid.reshape(1), wg, wu, wd, tm, tf)
    return [_combine_norm(sl, rec, h, final_g, ys, tb)
            for (_, h, tb), sl, rec in zip(groups, slots, recs)]


def _pick(total, pref):
    t = min(total, pref)
    while total % t:
        t //= 2
    return t


def kernel(x_prompt, x_sample, cache_k, cache_v, page_table, w_in, w_out, norm_mix_g, norm_ffn_g,
           lam_q1, lam_k1, lam_q2, lam_k2, subln_g, gm_ln_g, gm_ln_b, gm_ws, gm_bs,
           ffn_wg, ffn_wu, ffn_wd, moe_router, moe_wg, moe_wu, moe_wd, final_norm_g):
    b, s, d = x_prompt.shape
    db, t = x_sample.shape[:2]
    depth = w_in.shape[0]
    n_pages = page_table.shape[1]
    past = n_pages * PAGE_SIZE
    n_p, n_s = b * s, db * t
    assert depth == 2 and s % CHUNK == 0

    tm_p = _pick(s, 512)
    blk = _pick(s, 512)
    pages_per_step = _pick(n_pages, 8)

    hp = x_prompt.reshape(n_p, d)
    hs = x_sample.transpose(1, 0, 2).reshape(n_s, d)
    tab_p = _rope_tables(jnp.arange(s, dtype=F32))
    pos_s = jnp.arange(t, dtype=F32) + jnp.float32(past)
    tab_s = _rope_tables(jnp.repeat(pos_s, db))
    ck = cache_k.reshape(cache_k.shape[0], cache_k.shape[1], PAGE_SIZE * DA_HEADS, LANES)
    cv = cache_v.reshape(cache_v.shape[0], cache_v.shape[1], PAGE_SIZE * DA_HEADS, LANES)
    lane = jnp.arange(LANES)
    comp_mask = jnp.stack([lane < DA_HEAD_DIM, lane >= DA_HEAD_DIM])
    new_rows = PAGE_SIZE
    assert t * DA_HEADS <= new_rows

    outs = {k: [] for k in ("kp", "vp", "ks", "vs", "gv")}
    y_p = y_s = None
    for l in range(depth):
        lam_init = 0.8 - 0.6 * math.exp(-0.3 * l)
        w_in_bf = w_in[l].astype(BF16)
        w_out_bf = w_out[l].astype(BF16)
        g_mix = norm_mix_g[l][None]
        g_ffn = norm_ffn_g[l][None]
        lng, lnb = gm_ln_g[l][None], gm_ln_b[l][None]
        lams = (lam_q1[l][None], lam_k1[l][None], lam_q2[l][None], lam_k2[l][None])
        sg = subln_g[l][None]

        q, k, v, kb, vb, u, gv = _proj(hp, g_mix, w_in_bf, tab_p, lng, lnb, tm_p, BF16, BF16)
        a = _attn_prompt(q.reshape(b, s, QK_COLS), kb.reshape(b, s, QK_COLS),
                         vb.reshape(b, s, 2 * DA_WIDTH), lams, sg, lam_init, blk)
        hp, xn_p = _mix_out(a.reshape(n_p, DA_WIDTH), u, gv, hp, gm_ws[l], gm_bs[l], w_out_bf, g_ffn, tm_p)

        sq, sk, sv, skb, svb, su, sgv = _proj(hs, g_mix, w_in_bf, tab_s, lng, lnb, n_s, F32, F32)
        q5 = sq.reshape(t, db, DA_HEADS, 1, LANES).transpose(1, 2, 3, 0, 4)
        q_rows = jnp.where(comp_mask[None, None, :, None, :], q5, jnp.zeros((), BF16))
        q_rows = q_rows.reshape(db, DA_HEADS * 2 * t, LANES)
        pad = ((0, 0), (0, new_rows - t * DA_HEADS), (0, 0))
        kn = jnp.pad(skb.reshape(t, db, DA_HEADS, LANES).transpose(1, 0, 2, 3).reshape(db, t * DA_HEADS, LANES), pad)
        svb = svb.reshape(t, db, DA_HEADS, 2 * DA_V_DIM)[..., :DA_V_DIM]
        vn = jnp.pad(svb.transpose(1, 0, 2, 3).reshape(db, t * DA_HEADS, LANES), pad)
        sa = _attn_sample(l, q_rows, kn, vn, ck, cv, page_table, lams, sg, lam_init, t, pages_per_step)
        sa = sa.transpose(1, 0, 2).reshape(n_s, DA_WIDTH).astype(BF16)
        hs, xn_s = _mix_out(sa, su, sgv, hs, gm_ws[l], gm_bs[l], w_out_bf, g_ffn, n_s, sample_shape=(t, db))

        outs["kp"].append(k.reshape(b, s, DA_HEADS, 2 * DA_HEAD_DIM))
        outs["vp"].append(v.reshape(b, s, DA_HEADS, DA_V_DIM))
        outs["ks"].append(sk.reshape(t, db, DA_HEADS, 2 * DA_HEAD_DIM).transpose(1, 0, 2, 3))
        outs["vs"].append(sv.reshape(t, db, DA_HEADS, DA_V_DIM).transpose(1, 0, 2, 3))
        outs["gv"].append(sgv.reshape(t, db, GM_GROUPS, GM_CH).transpose(1, 0, 2, 3))

        j = l // 2
        if l % 2 == 0:
            wg, wu, wd = ffn_wg[j].astype(BF16), ffn_wu[j].astype(BF16), ffn_wd[j].astype(BF16)
            tf = _pick(wg.shape[1], 1408)
            hp = _ffn(xn_p, hp, wg, wu, wd, tm_p, tf)
            hs = _ffn(xn_s, hs, wg, wu, wd, n_s, tf)
        else:
            wg, wu, wd = moe_wg[j].astype(BF16), moe_wu[j].astype(BF16), moe_wd[j].astype(BF16)
            y_p, y_s = _moe_final([(xn_p, hp, _pick(n_p, 256)), (xn_s, hs, n_s)], moe_router[j],
                                  wg, wu, wd, final_norm_g[None], 512, _pick(wg.shape[2], 512))

    y_prompt = y_p.reshape(b, s, d)
    y_sample = y_s.reshape(t, db, d).transpose(1, 0, 2)
    return (y_prompt, y_sample, jnp.stack(outs["kp"]), jnp.stack(outs["vp"]),
            jnp.stack(outs["ks"]), jnp.stack(outs["vs"]), jnp.stack(outs["gv"]))
```

```python
import functools
import math

import jax
import jax.numpy as jnp
from jax import lax
from jax.experimental import pallas as pl
from jax.experimental.pallas import tpu as pltpu

F32 = jnp.float32
BF16 = jnp.bfloat16

DA_HEADS = 4
DA_HEAD_DIM = 64
DA_V_DIM = 2 * DA_HEAD_DIM
QK_COLS = DA_HEADS * 2 * DA_HEAD_DIM
DA_WIDTH = DA_HEADS * DA_V_DIM
ROT_DIM = DA_HEAD_DIM // 4
ROPE_THETA = 500000.0
GM_GROUPS = 4
GM_CH = 128
GM_WIDTH = GM_GROUPS * GM_CH
CHUNK = 128
PAGE_SIZE = 128
TOP_K = 2
RMS_EPS = 1e-6
SUBLN_EPS = 1e-5
LN_EPS = 1e-5
NEG_BIG = -1e30
LOG2_E = math.log2(math.e)
LANES = 128

VMEM_LIMIT = 52 * 1024 * 1024


def _cparams(sem):
    return pltpu.CompilerParams(dimension_semantics=sem, vmem_limit_bytes=VMEM_LIMIT)


def _nt_dot(a, b):
    return lax.dot_general(a, b, (((1,), (1,)), ((), ())), preferred_element_type=F32)


def _rmsnorm_rows(x, g, eps):
    return x * lax.rsqrt(jnp.mean(x * x, axis=-1, keepdims=True) + eps) * g


def _tiles_per_row(d):
    return d // LANES


def _put_rows(ref, x):
    n, d = x.shape
    if ref.shape == x.shape:
        ref[...] = x
        return
    tpr = _tiles_per_row(d)
    assert ref.shape == (n * tpr, LANES)
    for c in range(tpr):
        ref[pl.ds(c, n, stride=tpr), :] = x[:, c * LANES:(c + 1) * LANES]


def _get_rows(ref, n, d):
    tpr = _tiles_per_row(d)
    return jnp.concatenate([ref[pl.ds(c, n, stride=tpr), :] for c in range(tpr)], axis=1)


def _gelu_exact(x):
    return 0.5 * x * (1.0 + lax.erf(x * (2.0 ** -0.5)))


def _diff_lambda(lq1, lk1, lq2, lk2, lam_init):
    a = jnp.exp(jnp.sum(lq1 * lk1, axis=-1, keepdims=True))
    b = jnp.exp(jnp.sum(lq2 * lk2, axis=-1, keepdims=True))
    return a - b + lam_init


def _put_head(ref, t, x):
    if ref.shape[-1] == DA_HEADS * LANES:
        ref[:, t * LANES:(t + 1) * LANES] = x
    else:
        ref.at[0][pl.ds(t, x.shape[0], stride=DA_HEADS), :] = x


def _proj_body(n_alias, h_ref, g_ref, w_ref, cos_ref, sa_ref, sb_ref, lng_ref, lnb_ref, *rest):
    q_ref, k_ref, v_ref, kb_ref, vb_ref, u_ref, gv_ref = rest[n_alias:]
    h = h_ref[...]
    xn = _rmsnorm_rows(h, g_ref[...], RMS_EPS).astype(BF16)
    z = jnp.dot(xn, w_ref[...], preferred_element_type=F32)
    cos, sa, sb = cos_ref[...], sa_ref[...], sb_ref[...]

    def rope(x):
        return (x * cos + pltpu.roll(x, ROT_DIM // 2, 1) * sa
                + pltpu.roll(x, LANES - ROT_DIM // 2, 1) * sb)

    scale = DA_HEAD_DIM ** -0.5 * LOG2_E
    ones = jnp.ones((h.shape[0], DA_V_DIM), vb_ref.dtype)
    for t in range(DA_HEADS):
        sl = slice(t * LANES, (t + 1) * LANES)
        q_ref[:, sl] = (rope(z[:, sl]) * scale).astype(q_ref.dtype)
        kt = rope(z[:, QK_COLS + t * LANES:QK_COLS + (t + 1) * LANES])
        _put_head(k_ref, t, kt)
        kb_ref[:, sl] = kt.astype(kb_ref.dtype)
        vt = z[:, 2 * QK_COLS + t * DA_V_DIM:2 * QK_COLS + (t + 1) * DA_V_DIM]
        _put_head(v_ref, t, vt)
        vb_ref[:, 2 * t * DA_V_DIM:(2 * t + 1) * DA_V_DIM] = vt.astype(vb_ref.dtype)
        vb_ref[:, (2 * t + 1) * DA_V_DIM:(2 * t + 2) * DA_V_DIM] = ones
    gz = _gelu_exact(z[:, 2 * QK_COLS + DA_WIDTH:])
    u_ref[...] = gz[:, :GM_WIDTH].astype(u_ref.dtype)
    gvr = gz[:, GM_WIDTH:]
    mu = jnp.mean(gvr, axis=-1, keepdims=True)
    d = gvr - mu
    var = jnp.mean(d * d, axis=-1, keepdims=True)
    gv = d * lax.rsqrt(var + LN_EPS) * lng_ref[...] + lnb_ref[...]
    gv_ref[...] = gv.astype(gv_ref.dtype)


def _proj(h, g, w_bf, tables, lng, lnb, tm, u_dtype, gv_dtype, kv_stack=None):
    m, d = h.shape
    cos_t, sa_t, sb_t = tables
    ntab = cos_t.shape[0] // tm
    row = lambda i: (i, 0)
    fixed = lambda i: (0, 0)
    tab = lambda i: (i % ntab, 0)
    wide = lambda dt: jax.ShapeDtypeStruct((m, QK_COLS), dt)
    in_specs = [
        pl.BlockSpec((tm, d), row),
        pl.BlockSpec((1, d), fixed),
        pl.BlockSpec(w_bf.shape, fixed),
        pl.BlockSpec((tm, LANES), tab),
        pl.BlockSpec((tm, LANES), tab),
        pl.BlockSpec((tm, LANES), tab),
        pl.BlockSpec((1, GM_WIDTH), fixed),
        pl.BlockSpec((1, GM_WIDTH), fixed),
    ]
    args = [h, g, w_bf, cos_t, sa_t, sb_t, lng, lnb]
    kv_spec = pl.BlockSpec((tm, QK_COLS), row)
    kv_shape = wide(F32)
    aliases = {}
    if kv_stack is not None:
        layer, k_all, v_all = kv_stack
        kv_spec = pl.BlockSpec((1, tm * DA_HEADS, LANES), lambda i: (layer, i, 0))
        kv_shape = jax.ShapeDtypeStruct(k_all.shape, F32)
        aliases = {len(args): 1, len(args) + 1: 2}
        in_specs += [pl.BlockSpec(memory_space=pl.ANY)] * 2
        args += [k_all, v_all]
    return pl.pallas_call(
        functools.partial(_proj_body, len(aliases)),
        grid=(m // tm,),
        in_specs=in_specs,
        out_specs=[pl.BlockSpec((tm, QK_COLS), row), kv_spec, kv_spec,
                   pl.BlockSpec((tm, QK_COLS), row), pl.BlockSpec((tm, 2 * DA_WIDTH), row),
                   pl.BlockSpec((tm, GM_WIDTH), row), pl.BlockSpec((tm, GM_WIDTH), row)],
        out_shape=[wide(BF16), kv_shape, kv_shape, wide(BF16),
                   jax.ShapeDtypeStruct((m, 2 * DA_WIDTH), BF16), wide(u_dtype), wide(gv_dtype)],
        input_output_aliases=aliases,
        compiler_params=_cparams(("parallel",)),
        name="proj",
    )(*args)


def _rope_tables(pos):
    inv = ROPE_THETA ** (-jnp.arange(0, ROT_DIM, 2, dtype=F32) / ROT_DIM)
    ang = pos[:, None] * inv[None, :]
    c, s = jnp.cos(ang), jnp.sin(ang)
    j = jnp.arange(LANES) % DA_HEAD_DIM
    first = j < ROT_DIM // 2
    second = (j >= ROT_DIM // 2) & (j < ROT_DIM)
    idx = j % (ROT_DIM // 2)
    cg, sg = c[:, idx], s[:, idx]
    cos_t = jnp.where(first | second, cg, 1.0)
    sa_t = jnp.where(second, sg, 0.0)
    sb_t = jnp.where(first, -sg, 0.0)
    return cos_t, sa_t, sb_t


def _attn_prompt_body(lam_init, blk, lq1_ref, lk1_ref, lq2_ref, lk2_ref, sg_ref,
                      q_ref, k_ref, v_ref, o_ref, s_a, s_b, mx_a, mx_b, acc_sc):
    n_q = q_ref.shape[1] // blk
    buf_a, buf_b = (s_a, mx_a), (s_b, mx_b)
    lane = lax.broadcasted_iota(jnp.int32, (blk, LANES), 1)
    r2 = lax.broadcasted_iota(jnp.int32, (2 * blk, blk), 0)
    c2 = lax.broadcasted_iota(jnp.int32, (2 * blk, blk), 1)
    causal = c2 <= jnp.where(r2 >= blk, r2 - blk, r2)
    lam = _diff_lambda(lq1_ref[...], lk1_ref[...], lq2_ref[...], lk2_ref[...], lam_init)
    gain = sg_ref[...] * (1.0 - lam_init)

    def fold(s):
        out = s[:, :LANES]
        for i in range(1, blk // LANES):
            out = jnp.maximum(out, s[:, i * LANES:(i + 1) * LANES])
        return out

    def stacked_q(qi):
        q = q_ref[0, pl.ds(pl.multiple_of(qi * blk, blk), blk), :]
        zero = jnp.zeros_like(q)
        return jnp.concatenate([jnp.where(lane < DA_HEAD_DIM, q, zero),
                                jnp.where(lane >= DA_HEAD_DIM, q, zero)], axis=0)

    def score_tile(q_st, buf, j):
        s_ref, mx_ref = buf
        s = _nt_dot(q_st, k_ref[0, pl.ds(pl.multiple_of(j * blk, blk), blk), :])
        s_ref[j] = s
        mx_ref[...] = jnp.maximum(mx_ref[...], fold(s))

    def diag_tile(qi, buf):
        s_ref, mx_ref = buf
        q0 = pl.multiple_of(qi * blk, blk)
        s = jnp.where(causal, _nt_dot(stacked_q(qi), k_ref[0, pl.ds(q0, blk), :]), NEG_BIG)
        s_ref[qi] = s
        m = jnp.max(jnp.maximum(mx_ref[...], fold(s)), axis=1, keepdims=True)
        mx_ref[...] = jnp.broadcast_to(m, (2 * blk, LANES))

    def prob_tile(buf, j):
        s_ref, mx_ref = buf
        mb = mx_ref[...]
        p = jnp.exp2(s_ref[j] - jnp.concatenate([mb] * (blk // LANES), axis=1)).astype(BF16)
        acc_sc[...] += jnp.dot(p, v_ref[0, pl.ds(pl.multiple_of(j * blk, blk), blk), :],
                               preferred_element_type=F32)

    def finish(qi):
        acc = acc_sc[...]
        o = (acc[:blk, :DA_V_DIM] / acc[:blk, DA_V_DIM:]
             - lam * (acc[blk:, :DA_V_DIM] / acc[blk:, DA_V_DIM:]))
        o = _rmsnorm_rows(o, gain, SUBLN_EPS)
        o_ref[0, pl.ds(pl.multiple_of(qi * blk, blk), blk), :] = o.astype(o_ref.dtype)

    neg = jnp.full((2 * blk, LANES), NEG_BIG, F32)

    def fused_block(qi, cur, nxt):
        q_next = stacked_q(qi + 1)
        nxt[1][...] = neg
        acc_sc[...] = jnp.zeros(acc_sc.shape, F32)

        def tile(j, c):
            prob_tile(cur, j)
            score_tile(q_next, nxt, j)
            return c

        lax.fori_loop(0, qi + 1, tile, 0)
        finish(qi)
        diag_tile(qi + 1, nxt)

    mx_a[...] = neg
    diag_tile(0, buf_a)

    def pair(i2, carry):
        fused_block(2 * i2, buf_a, buf_b)
        fused_block(2 * i2 + 1, buf_b, buf_a)
        return carry

    n_fused = n_q - 1
    lax.fori_loop(0, n_fused // 2, pair, 0)
    if n_fused % 2:
        fused_block(n_fused - 1, buf_a, buf_b)
    last = buf_b if n_fused % 2 else buf_a
    acc_sc[...] = jnp.zeros(acc_sc.shape, F32)

    def tail(j, c):
        prob_tile(last, j)
        return c

    lax.fori_loop(0, n_q, tail, 0)
    finish(n_q - 1)


def _attn_prompt(q, kb, vb1, lams, sg, lam_init, blk):
    b, s, _ = q.shape
    small = pl.BlockSpec((1, DA_HEAD_DIM), lambda bi, h: (0, 0))
    per_head = pl.BlockSpec((1, s, LANES), lambda bi, h: (bi, 0, h))
    return pl.pallas_call(
        functools.partial(_attn_prompt_body, lam_init, blk),
        grid=(b, DA_HEADS),
        in_specs=[small, small, small, small,
                  pl.BlockSpec((1, DA_V_DIM), lambda bi, h: (0, 0)),
                  per_head, per_head,
                  pl.BlockSpec((1, s, 2 * DA_V_DIM), lambda bi, h: (bi, 0, h))],
        out_specs=per_head,
        out_shape=jax.ShapeDtypeStruct((b, s, DA_WIDTH), BF16),
        scratch_shapes=[pltpu.VMEM((s // blk, 2 * blk, blk), F32),
                        pltpu.VMEM((s // blk, 2 * blk, blk), F32),
                        pltpu.VMEM((2 * blk, LANES), F32),
                        pltpu.VMEM((2 * blk, LANES), F32),
                        pltpu.VMEM((2 * blk, 2 * DA_V_DIM), F32)],
        compiler_params=_cparams(("parallel", "parallel")),
        name="attn_prompt",
    )(*lams, sg, q, kb, vb1)


def _attn_sample_body(lam_init, n_tok, pages_per_step, pt_ref, lq1_ref, lk1_ref, lq2_ref, lk2_ref,
                      sg_ref, q_ref, kn_ref, vn_ref, *rest):
    kp = rest[:pages_per_step]
    vp = rest[pages_per_step:2 * pages_per_step]
    o_ref = rest[2 * pages_per_step]
    m_sc, l_sc, a_sc = rest[2 * pages_per_step + 1:]
    j = pl.program_id(1)
    nrow = DA_HEADS * 2 * n_tok
    ncol = PAGE_SIZE * DA_HEADS

    @pl.when(j == 0)
    def _():
        m_sc[...] = jnp.full(m_sc.shape, NEG_BIG, F32)
        l_sc[...] = jnp.zeros(l_sc.shape, F32)
        a_sc[...] = jnp.zeros(a_sc.shape, F32)

    q = q_ref[0]
    row = lax.broadcasted_iota(jnp.int32, (nrow, ncol), 0)
    colk = lax.broadcasted_iota(jnp.int32, (nrow, ncol), 1)
    same_head = (colk % DA_HEADS) == (row // (2 * n_tok))

    def update(s_list, v_list):
        m = m_sc[...]
        mn = m
        for s in s_list:
            mn = jnp.maximum(mn, jnp.max(s, axis=-1, keepdims=True))
        alpha = jnp.exp2(m - mn)
        l = alpha * l_sc[...]
        a = alpha * a_sc[...]
        for s, v in zip(s_list, v_list):
            p = jnp.exp2(s - mn)
            l = l + jnp.sum(p, axis=-1, keepdims=True)
            a = a + jnp.dot(p.astype(BF16), v, preferred_element_type=F32)
        m_sc[...] = mn
        l_sc[...] = l
        a_sc[...] = a

    s_list, v_list = [], []
    for i in range(pages_per_step):
        kb = kp[i][0, 0].astype(BF16)
        s_list.append(jnp.where(same_head, _nt_dot(q, kb), NEG_BIG))
        v_list.append(vp[i][0, 0].astype(BF16))
    update(s_list, v_list)

    @pl.when(j == pl.num_programs(1) - 1)
    def _():
        rown = lax.broadcasted_iota(jnp.int32, (nrow, kn_ref.shape[1]), 0)
        coln = lax.broadcasted_iota(jnp.int32, (nrow, kn_ref.shape[1]), 1)
        ok = ((coln < n_tok * DA_HEADS) & ((coln % DA_HEADS) == (rown // (2 * n_tok)))
              & ((coln // DA_HEADS) <= (rown % n_tok)))
        s = jnp.where(ok, _nt_dot(q, kn_ref[0]), NEG_BIG)
        update([s], [vn_ref[0]])
        a = a_sc[...] / l_sc[...]
        lam = _diff_lambda(lq1_ref[...], lk1_ref[...], lq2_ref[...], lk2_ref[...], lam_init)
        for h in range(DA_HEADS):
            base = h * 2 * n_tok
            o = a[base:base + n_tok] - lam * a[base + n_tok:base + 2 * n_tok]
            o = _rmsnorm_rows(o, sg_ref[...], SUBLN_EPS) * (1.0 - lam_init)
            o_ref[0, :, h * LANES:(h + 1) * LANES] = o


def _attn_sample(layer, q_rows, kn, vn, cache_k, cache_v, page_table, lams, sg, lam_init,
                 n_tok, pages_per_step):
    db, nrow, _ = q_rows.shape
    n_pages = page_table.shape[1]
    assert n_pages % pages_per_step == 0
    pt = page_table.reshape(-1)
    small = pl.BlockSpec((1, DA_HEAD_DIM), lambda b, j, p: (0, 0))
    per_seq = lambda b, j, p: (b, 0, 0)

    def page_spec(i):
        return pl.BlockSpec(
            (1, 1, PAGE_SIZE * DA_HEADS, LANES),
            lambda b, j, p: (layer, p[b * n_pages + j * pages_per_step + i], 0, 0))

    pages = [page_spec(i) for i in range(pages_per_step)]
    grid_spec = pltpu.PrefetchScalarGridSpec(
        num_scalar_prefetch=1,
        grid=(db, n_pages // pages_per_step),
        in_specs=[small, small, small, small,
                  pl.BlockSpec((1, DA_V_DIM), lambda b, j, p: (0, 0)),
                  pl.BlockSpec((1, nrow, LANES), per_seq),
                  pl.BlockSpec((1,) + kn.shape[1:], per_seq),
                  pl.BlockSpec((1,) + vn.shape[1:], per_seq)] + pages + pages,
        out_specs=pl.BlockSpec((1, n_tok, DA_WIDTH), per_seq),
        scratch_shapes=[pltpu.VMEM((nrow, 1), F32), pltpu.VMEM((nrow, 1), F32),
                        pltpu.VMEM((nrow, DA_V_DIM), F32)],
    )
    return pl.pallas_call(
        functools.partial(_attn_sample_body, lam_init, n_tok, pages_per_step),
        grid_spec=grid_spec,
        out_shape=jax.ShapeDtypeStruct((db, n_tok, DA_WIDTH), F32),
        compiler_params=_cparams(("parallel", "arbitrary")),
        name="attn_sample",
    )(pt, *lams, sg, q_rows, kn, vn, *([cache_k] * pages_per_step), *([cache_v] * pages_per_step))


def _mix_out_prompt_body(n_chunk, a_ref, u_ref, gv_ref, h_ref, ws_ref, bst_ref, wo_ref, gf_ref,
                         ho_ref, xn_ref, g_sc):
    r = lax.broadcasted_iota(jnp.int32, (CHUNK, CHUNK), 0)
    c = lax.broadcasted_iota(jnp.int32, (CHUNK, CHUNK), 1)
    for g in range(GM_GROUPS):
        w = jnp.where(c <= r, ws_ref[g], 0.0).astype(BF16)
        bias = bst_ref[:, g:g + 1]
        lanes = slice(g * GM_CH, (g + 1) * GM_CH)
        for ci in range(n_chunk):
            rows = slice(ci * CHUNK, (ci + 1) * CHUNK)
            m = jnp.dot(w, gv_ref[rows, lanes], preferred_element_type=F32) + bias
            g_sc[rows, lanes] = (u_ref[rows, lanes].astype(F32) * m).astype(BF16)
    out = (jnp.dot(a_ref[...], wo_ref[:DA_WIDTH, :], preferred_element_type=F32)
           + jnp.dot(g_sc[...], wo_ref[DA_WIDTH:, :], preferred_element_type=F32))
    hn = h_ref[...] + out
    ho_ref[...] = hn
    _put_rows(xn_ref, _rmsnorm_rows(hn, gf_ref[...], RMS_EPS))


def _mix_out_sample_body(n_tok, n_seq, a_ref, u_ref, gv_ref, h_ref, ws_ref, bs_ref, wo_ref, gf_ref,
                         ho_ref, xn_ref, g_sc):
    for g in range(GM_GROUPS):
        lanes = slice(g * GM_CH, (g + 1) * GM_CH)
        for t in range(n_tok):
            m = jnp.zeros((n_seq, GM_CH), F32) + bs_ref[g:g + 1, t:t + 1]
            for s in range(t + 1):
                m = m + ws_ref[g, t:t + 1, s:s + 1] * gv_ref[s * n_seq:(s + 1) * n_seq, lanes]
            rows = slice(t * n_seq, (t + 1) * n_seq)
            g_sc[rows, lanes] = u_ref[rows, lanes] * m
    out = (jnp.dot(a_ref[...], wo_ref[:DA_WIDTH, :], preferred_element_type=F32)
           + jnp.dot(g_sc[...].astype(BF16), wo_ref[DA_WIDTH:, :], preferred_element_type=F32))
    hn = h_ref[...] + out
    ho_ref[...] = hn
    _put_rows(xn_ref, _rmsnorm_rows(hn, gf_ref[...], RMS_EPS))


def _mix_out(a, u, gv, h, ws, bs, wo_bf, gf, tm, sample_shape=None, tile_rows=False):
    m, d = h.shape
    row = lambda i: (i, 0)
    tpr = _tiles_per_row(d) if tile_rows else 1
    fixed2 = lambda i: (0, 0)
    fixed3 = lambda i: (0, 0, 0)
    if sample_shape is None:
        body = functools.partial(_mix_out_prompt_body, tm // CHUNK)
        bias = bs.T
        g_dtype = BF16
    else:
        body = functools.partial(_mix_out_sample_body, *sample_shape)
        bias = bs
        g_dtype = F32
    return pl.pallas_call(
        body,
        grid=(m // tm,),
        in_specs=[pl.BlockSpec((tm, DA_WIDTH), row),
                  pl.BlockSpec((tm, GM_WIDTH), row),
                  pl.BlockSpec((tm, GM_WIDTH), row),
                  pl.BlockSpec((tm, d), row),
                  pl.BlockSpec(ws.shape, fixed3),
                  pl.BlockSpec(bias.shape, fixed2),
                  pl.BlockSpec(wo_bf.shape, fixed2),
                  pl.BlockSpec((1, d), fixed2)],
        out_specs=[pl.BlockSpec((tm, d), row), pl.BlockSpec((tm * tpr, d // tpr), row)],
        out_shape=[jax.ShapeDtypeStruct((m, d), F32),
                   jax.ShapeDtypeStruct((m * tpr, d // tpr), F32)],
        scratch_shapes=[pltpu.VMEM((tm, GM_WIDTH), g_dtype)],
        compiler_params=_cparams(("parallel",)),
        name="mix_out",
    )(a, u, gv, h, ws, bias, wo_bf, gf)


def _ffn_body(xn_ref, h_ref, wg_ref, wu_ref, wd_ref, o_ref, acc_sc):
    f = pl.program_id(1)

    @pl.when(f == 0)
    def _():
        acc_sc[...] = jnp.zeros(acc_sc.shape, F32)

    x = xn_ref[...].astype(BF16)
    g = jnp.dot(x, wg_ref[...], preferred_element_type=F32)
    u = jnp.dot(x, wu_ref[...], preferred_element_type=F32)
    hid = (g * jax.nn.sigmoid(g) * u).astype(BF16)
    acc_sc[...] += jnp.dot(hid, wd_ref[...], preferred_element_type=F32)

    @pl.when(f == pl.num_programs(1) - 1)
    def _():
        o_ref[...] = h_ref[...] + acc_sc[...]


def _ffn(xn, h, wg, wu, wd, tm, tf):
    m, d = h.shape
    ff = wg.shape[1]
    return pl.pallas_call(
        _ffn_body,
        grid=(m // tm, ff // tf),
        in_specs=[pl.BlockSpec((tm, d), lambda i, f: (i, 0)),
                  pl.BlockSpec((tm, d), lambda i, f: (i, 0)),
                  pl.BlockSpec((d, tf), lambda i, f: (0, f)),
                  pl.BlockSpec((d, tf), lambda i, f: (0, f)),
                  pl.BlockSpec((tf, d), lambda i, f: (f, 0))],
        out_specs=pl.BlockSpec((tm, d), lambda i, f: (i, 0)),
        out_shape=jax.ShapeDtypeStruct((m, d), F32),
        scratch_shapes=[pltpu.VMEM((tm, d), F32)],
        compiler_params=_cparams(("parallel", "arbitrary")),
        name="ffn",
    )(xn, h, wg, wu, wd)


R_E1, R_E2, R_G1, R_G2, R_RANK1, R_RANK2 = range(6)


def _router_body(n_exp, xn_ref, wr_ref, base_ref, o_ref, cnt_ref, cnt_sc):
    i = pl.program_id(0)

    @pl.when(i == 0)
    def _():
        cnt_sc[...] = base_ref[...]

    d = wr_ref.shape[0]
    tm = xn_ref.shape[0] // _tiles_per_row(d)
    logits = jnp.dot(_get_rows(xn_ref, tm, d), wr_ref[...], preferred_element_type=F32,
                     precision=lax.Precision.HIGHEST)
    col = lax.broadcasted_iota(jnp.int32, logits.shape, 1)
    logits = jnp.where(col < n_exp, logits, -jnp.inf)
    big = jnp.int32(LANES)
    v1 = jnp.max(logits, axis=-1, keepdims=True)
    i1 = jnp.min(jnp.where(logits == v1, col, big), axis=-1, keepdims=True)
    rest = jnp.where(col == i1, -jnp.inf, logits)
    v2 = jnp.max(rest, axis=-1, keepdims=True)
    i2 = jnp.min(jnp.where(rest == v2, col, big), axis=-1, keepdims=True)
    e = jnp.exp(v2 - v1)
    g1 = 1.0 / (1.0 + e)
    g2 = e / (1.0 + e)
    hit = (col == i1) | (col == i2)
    r = lax.broadcasted_iota(jnp.int32, (tm, tm), 0)
    c = lax.broadcasted_iota(jnp.int32, (tm, tm), 1)
    below = jnp.where(c < r, 1.0, 0.0).astype(BF16)
    prefix = jnp.dot(below, jnp.where(hit, 1.0, 0.0).astype(BF16), preferred_element_type=F32)
    rank = prefix + cnt_sc[...]
    r1 = jnp.sum(jnp.where(col == i1, rank, 0.0), axis=-1, keepdims=True)
    r2 = jnp.sum(jnp.where(col == i2, rank, 0.0), axis=-1, keepdims=True)
    cnt_sc[...] += jnp.sum(jnp.where(hit, 1.0, 0.0), axis=0, keepdims=True)
    rec = jnp.zeros(logits.shape, F32)
    for idx, val in ((R_E1, i1.astype(F32)), (R_E2, i2.astype(F32)), (R_G1, g1), (R_G2, g2),
                     (R_RANK1, r1), (R_RANK2, r2)):
        rec = jnp.where(col == idx, val, rec)
    o_ref[...] = rec
    cnt_ref[...] = cnt_sc[...]


def _router(xn_t, wr_pad, base_counts, n_exp, tm):
    tpr = _tiles_per_row(wr_pad.shape[0])
    m = xn_t.shape[0] // tpr
    return pl.pallas_call(
        functools.partial(_router_body, n_exp),
        grid=(m // tm,),
        in_specs=[pl.BlockSpec((tm * tpr, LANES), lambda i: (i, 0)),
                  pl.BlockSpec(wr_pad.shape, lambda i: (0, 0)),
                  pl.BlockSpec((1, LANES), lambda i: (0, 0))],
        out_specs=[pl.BlockSpec((tm, LANES), lambda i: (i, 0)),
                   pl.BlockSpec((1, LANES), lambda i: (0, 0))],
        out_shape=[jax.ShapeDtypeStruct((m, LANES), F32), jax.ShapeDtypeStruct((1, LANES), F32)],
        scratch_shapes=[pltpu.VMEM((1, LANES), F32)],
        compiler_params=_cparams(("arbitrary",)),
        name="router",
    )(xn_t, wr_pad, base_counts)


DMA_UNROLL = 8


def _tile_copy(src, src_row, dst, dst_row, tpr, sem):
    return pltpu.make_async_copy(src.at[pl.ds(pl.multiple_of(src_row * tpr, tpr), tpr), :],
                                 dst.at[pl.ds(pl.multiple_of(dst_row * tpr, tpr), tpr), :], sem)


def _dispatch_body(tm, tpr, slot_ref, xn_ref, xs_in_ref, xs_ref, sem):
    del xs_in_ref

    def start(t, c):
        for k in range(TOP_K):
            _tile_copy(xn_ref, t, xs_ref, slot_ref[0, 0, TOP_K * t + k], tpr, sem).start()
        return c

    def wait(t, c):
        for k in range(TOP_K):
            _tile_copy(xn_ref, t, xs_ref, t, tpr, sem).wait()
        return c

    lax.fori_loop(0, tm, start, 0, unroll=DMA_UNROLL)
    lax.fori_loop(0, tm, wait, 0, unroll=DMA_UNROLL)


def _dispatch(slots, xn_t, xs_t, tm, tpr):
    m = xn_t.shape[0] // tpr
    slots3 = slots.reshape(m // tm, 1, tm * TOP_K)
    return pl.pallas_call(
        functools.partial(_dispatch_body, tm, tpr),
        grid=(m // tm,),
        in_specs=[pl.BlockSpec((1, 1, tm * TOP_K), lambda i: (i, 0, 0), memory_space=pltpu.SMEM),
                  pl.BlockSpec((tm * tpr, LANES), lambda i: (i, 0)),
                  pl.BlockSpec(memory_space=pl.ANY)],
        out_specs=pl.BlockSpec(memory_space=pl.ANY),
        out_shape=jax.ShapeDtypeStruct(xs_t.shape, xs_t.dtype),
        scratch_shapes=[pltpu.SemaphoreType.DMA(())],
        input_output_aliases={2: 0},
        compiler_params=_cparams(("arbitrary",)),
        name="moe_dispatch",
    )(slots3, xn_t, xs_t)


def _combine_body(tm, tpr, slot_cur_ref, slot_nxt_ref, rec_ref, h_ref, g_ref, ys_ref, o_ref,
                  buf, sem):
    i = pl.program_id(0)
    n = pl.num_programs(0)
    d = h_ref.shape[1]

    def fetch(slot_ref, half):
        def start(t, c):
            for k in range(TOP_K):
                _tile_copy(ys_ref, slot_ref[0, 0, TOP_K * t + k], buf.at[half, k], t, tpr,
                           sem.at[half]).start()
            return c
        lax.fori_loop(0, tm, start, 0, unroll=DMA_UNROLL)

    def wait_block(half):
        def wait(t, c):
            for k in range(TOP_K):
                _tile_copy(ys_ref, t, buf.at[half, k], t, tpr, sem.at[half]).wait()
            return c
        lax.fori_loop(0, tm, wait, 0, unroll=DMA_UNROLL)

    def step(half):
        @pl.when(i == 0)
        def _():
            fetch(slot_cur_ref, half)

        @pl.when(i + 1 < n)
        def _():
            fetch(slot_nxt_ref, 1 - half)

        wait_block(half)
        rec = rec_ref[...]
        y = (rec[:, R_G1:R_G1 + 1] * _get_rows(buf.at[half, 0], tm, d)
             + rec[:, R_G2:R_G2 + 1] * _get_rows(buf.at[half, 1], tm, d))
        o_ref[...] = _rmsnorm_rows(h_ref[...] + y, g_ref[...], RMS_EPS)

    @pl.when(i % 2 == 0)
    def _():
        step(0)

    @pl.when(i % 2 == 1)
    def _():
        step(1)


def _combine_norm(slots, rec, h, g, ys_t, tm):
    m, d = h.shape
    tpr = _tiles_per_row(d)
    n = m // tm
    slots3 = slots.reshape(n, 1, tm * TOP_K)
    row = lambda i: (i, 0)
    return pl.pallas_call(
        functools.partial(_combine_body, tm, tpr),
        grid=(n,),
        in_specs=[pl.BlockSpec((1, 1, tm * TOP_K), lambda i: (i, 0, 0), memory_space=pltpu.SMEM),
                  pl.BlockSpec((1, 1, tm * TOP_K), lambda i: (jnp.minimum(i + 1, n - 1), 0, 0),
                               memory_space=pltpu.SMEM),
                  pl.BlockSpec((tm, LANES), row),
                  pl.BlockSpec((tm, d), row),
                  pl.BlockSpec((1, d), lambda i: (0, 0)),
                  pl.BlockSpec(memory_space=pl.ANY)],
        out_specs=pl.BlockSpec((tm, d), row),
        out_shape=jax.ShapeDtypeStruct((m, d), F32),
        scratch_shapes=[pltpu.VMEM((2, TOP_K, tm * tpr, LANES), F32),
                        pltpu.SemaphoreType.DMA((2,))],
        compiler_params=_cparams(("arbitrary",)),
        name="moe_combine_norm",
    )(slots3, slots3, rec, h, g, ys_t)


def _moe_body(tm, be_ref, nv_ref, xs_ref, wg_ref, wu_ref, wd_ref, o_ref, x_sc, acc_sc):
    i = pl.program_id(0)
    f = pl.program_id(1)
    last = pl.num_programs(1) - 1
    live = i < nv_ref[0]
    d = x_sc.shape[1]

    @pl.when(live & (f == 0))
    def _():
        acc_sc[...] = jnp.zeros(acc_sc.shape, F32)
        x_sc[...] = _get_rows(xs_ref, tm, d).astype(BF16)

    @pl.when(live)
    def _():
        x = x_sc[...]
        g = jnp.dot(x, wg_ref[0], preferred_element_type=F32)
        u = jnp.dot(x, wu_ref[0], preferred_element_type=F32)
        hid = (g * jax.nn.sigmoid(g) * u).astype(BF16)
        acc_sc[...] += jnp.dot(hid, wd_ref[0], preferred_element_type=F32)

    @pl.when(live & (f == last))
    def _():
        _put_rows(o_ref, acc_sc[...])

    @pl.when(jnp.logical_not(live) & (f == last))
    def _():
        o_ref[...] = jnp.zeros(o_ref.shape, F32)


def _moe_experts(xs_t, block_expert, n_valid, wg, wu, wd, tm, tf):
    d, ff = wg.shape[1], wg.shape[2]
    tpr = _tiles_per_row(d)
    p = xs_t.shape[0] // tpr
    rows = pl.BlockSpec((tm * tpr, LANES), lambda i, f, be, nv: (i, 0))
    grid_spec = pltpu.PrefetchScalarGridSpec(
        num_scalar_prefetch=2,
        grid=(p // tm, ff // tf),
        in_specs=[rows,
                  pl.BlockSpec((1, d, tf), lambda i, f, be, nv: (be[i], 0, f)),
                  pl.BlockSpec((1, d, tf), lambda i, f, be, nv: (be[i], 0, f)),
                  pl.BlockSpec((1, tf, d), lambda i, f, be, nv: (be[i], f, 0))],
        out_specs=rows,
        scratch_shapes=[pltpu.VMEM((tm, d), BF16), pltpu.VMEM((tm, d), F32)],
    )
    return pl.pallas_call(
        functools.partial(_moe_body, tm),
        grid_spec=grid_spec,
        out_shape=jax.ShapeDtypeStruct(xs_t.shape, F32),
        compiler_params=_cparams(("parallel", "arbitrary")),
        name="moe_experts",
    )(block_expert, n_valid, xs_t, wg, wu, wd)


def _moe_final(groups, w_router, wg, wu, wd, final_g, tm, tf):
    d = w_router.shape[0]
    n_exp = w_router.shape[1]
    n_tok = sum(h.shape[0] for _, h, _ in groups)
    wr_pad = jnp.zeros((d, LANES), F32).at[:, :n_exp].set(w_router)
    counts = jnp.zeros((1, LANES), F32)
    recs = []
    for xn, h, _ in groups:
        rec, counts = _router(xn, wr_pad, counts, n_exp, _pick(h.shape[0], 512))
        recs.append(rec)
    cnt = counts[0, :n_exp].astype(jnp.int32)
    padded = ((cnt + tm - 1) // tm) * tm
    group_end = jnp.cumsum(padded)
    group_start = group_end - padded
    n_slots = ((n_tok * TOP_K + n_exp * (tm - 1)) // tm) * tm
    n_blocks = n_slots // tm
    block_start = jnp.arange(n_blocks, dtype=jnp.int32) * tm
    n_valid = (group_end[-1] // tm).astype(jnp.int32)
    block_expert = jnp.sum((block_start[:, None] >= group_end[None, :]).astype(jnp.int32), axis=1)
    last_valid = jnp.sum((jnp.maximum(n_valid - 1, 0) * tm >= group_end).astype(jnp.int32))
    block_expert = jnp.minimum(jnp.where(block_start < n_valid * tm, block_expert, last_valid),
                               n_exp - 1).astype(jnp.int32)

    def slots_of(rec):
        e = rec[:, R_E1:R_E2 + 1].astype(jnp.int32)
        rank = rec[:, R_RANK1:R_RANK2 + 1].astype(jnp.int32)
        start = jnp.sum(jnp.where(e[..., None] == jnp.arange(n_exp), group_start, 0), axis=-1)
        return (start + rank).astype(jnp.int32)

    slots = [slots_of(rec) for rec in recs]
    tpr = _tiles_per_row(d)
    xs = jnp.zeros((n_slots * tpr, LANES), F32)
    for (xn, _, tb), sl in zip(groups, slots):
        xs = _dispatch(sl, xn, xs, tb, tpr)
    ys = _moe_experts(xs, block_expert, n_valid.reshape(1), wg, wu, wd, tm, tf)
    return [_combine_norm(sl, rec, h, final_g, ys, tb)
            for (_, h, tb), sl, rec in zip(groups, slots, recs)]


def _pick(total, pref):
    t = min(total, pref)
    while total % t:
        t //= 2
    return t


def kernel(x_prompt, x_sample, cache_k, cache_v, page_table, w_in, w_out, norm_mix_g, norm_ffn_g,
           lam_q1, lam_k1, lam_q2, lam_k2, subln_g, gm_ln_g, gm_ln_b, gm_ws, gm_bs,
           ffn_wg, ffn_wu, ffn_wd, moe_router, moe_wg, moe_wu, moe_wd, final_norm_g):
    b, s, d = x_prompt.shape
    db, t = x_sample.shape[:2]
    depth = w_in.shape[0]
    n_pages = page_table.shape[1]
    past = n_pages * PAGE_SIZE
    n_p, n_s = b * s, db * t
    assert depth == 2 and s % CHUNK == 0

    tm_p = _pick(s, 512)
    blk = _pick(s, 512)
    pages_per_step = _pick(n_pages, 16)

    hp = x_prompt.reshape(n_p, d)
    hs = x_sample.transpose(1, 0, 2).reshape(n_s, d)
    tab_p = _rope_tables(jnp.arange(s, dtype=F32))
    pos_s = jnp.arange(t, dtype=F32) + jnp.float32(past)
    tab_s = _rope_tables(jnp.repeat(pos_s, db))
    ck = cache_k.reshape(cache_k.shape[0], cache_k.shape[1], PAGE_SIZE * DA_HEADS, LANES)
    cv = cache_v.reshape(cache_v.shape[0], cache_v.shape[1], PAGE_SIZE * DA_HEADS, LANES)
    lane = jnp.arange(LANES)
    comp_mask = jnp.stack([lane < DA_HEAD_DIM, lane >= DA_HEAD_DIM])
    new_rows = PAGE_SIZE
    assert t * DA_HEADS <= new_rows

    outs = {k: [] for k in ("ks", "vs", "gv")}
    y_p = y_s = None
    k_all = jnp.zeros((depth, n_p * DA_HEADS, LANES), F32)
    v_all = jnp.zeros((depth, n_p * DA_HEADS, LANES), F32)
    for l in range(depth):
        lam_init = 0.8 - 0.6 * math.exp(-0.3 * l)
        w_in_bf = w_in[l].astype(BF16)
        w_out_bf = w_out[l].astype(BF16)
        g_mix = norm_mix_g[l][None]
        g_ffn = norm_ffn_g[l][None]
        lng, lnb = gm_ln_g[l][None], gm_ln_b[l][None]
        lams = (lam_q1[l][None], lam_k1[l][None], lam_q2[l][None], lam_k2[l][None])
        sg = subln_g[l][None]

        q, k_all, v_all, kb, vb, u, gv = _proj(hp, g_mix, w_in_bf, tab_p, lng, lnb, tm_p, BF16, BF16,
                                               kv_stack=(l, k_all, v_all))
        a = _attn_prompt(q.reshape(b, s, QK_COLS), kb.reshape(b, s, QK_COLS),
                         vb.reshape(b, s, 2 * DA_WIDTH), lams, sg, lam_init, blk)
        moe_layer = l % 2 == 1
        hp, xn_p = _mix_out(a.reshape(n_p, DA_WIDTH), u, gv, hp, gm_ws[l], gm_bs[l], w_out_bf, g_ffn,
                            tm_p, tile_rows=moe_layer)

        sq, sk, sv, skb, svb, su, sgv = _proj(hs, g_mix, w_in_bf, tab_s, lng, lnb, n_s, F32, F32)
        q5 = sq.reshape(t, db, DA_HEADS, 1, LANES).transpose(1, 2, 3, 0, 4)
        q_rows = jnp.where(comp_mask[None, None, :, None, :], q5, jnp.zeros((), BF16))
        q_rows = q_rows.reshape(db, DA_HEADS * 2 * t, LANES)
        pad = ((0, 0), (0, new_rows - t * DA_HEADS), (0, 0))
        kn = jnp.pad(skb.reshape(t, db, DA_HEADS, LANES).transpose(1, 0, 2, 3).reshape(db, t * DA_HEADS, LANES), pad)
        svb = svb.reshape(t, db, DA_HEADS, 2 * DA_V_DIM)[..., :DA_V_DIM]
        vn = jnp.pad(svb.transpose(1, 0, 2, 3).reshape(db, t * DA_HEADS, LANES), pad)
        sa = _attn_sample(l, q_rows, kn, vn, ck, cv, page_table, lams, sg, lam_init, t, pages_per_step)
        sa = sa.transpose(1, 0, 2).reshape(n_s, DA_WIDTH).astype(BF16)
        hs, xn_s = _mix_out(sa, su, sgv, hs, gm_ws[l], gm_bs[l], w_out_bf, g_ffn, n_s,
                            sample_shape=(t, db), tile_rows=moe_layer)

        outs["ks"].append(sk.reshape(t, db, DA_HEADS, 2 * DA_HEAD_DIM).transpose(1, 0, 2, 3))
        outs["vs"].append(sv.reshape(t, db, DA_HEADS, DA_V_DIM).transpose(1, 0, 2, 3))
        outs["gv"].append(sgv.reshape(t, db, GM_GROUPS, GM_CH).transpose(1, 0, 2, 3))

        j = l // 2
        if l % 2 == 0:
            wg, wu, wd = ffn_wg[j].astype(BF16), ffn_wu[j].astype(BF16), ffn_wd[j].astype(BF16)
            tf = _pick(wg.shape[1], 1408)
            hp = _ffn(xn_p, hp, wg, wu, wd, tm_p, tf)
            hs = _ffn(xn_s, hs, wg, wu, wd, n_s, tf)
        else:
            wg, wu, wd = moe_wg[j].astype(BF16), moe_wu[j].astype(BF16), moe_wd[j].astype(BF16)
            y_p, y_s = _moe_final([(xn_p, hp, _pick(n_p, 256)), (xn_s, hs, n_s)], moe_router[j],
                                  wg, wu, wd, final_norm_g[None], 512, _pick(wg.shape[2], 896))

    y_prompt = y_p.reshape(b, s, d)
    y_sample = y_s.reshape(t, db, d).transpose(1, 0, 2)
    return (y_prompt, y_sample,
            k_all.reshape(depth, b, s, DA_HEADS, 2 * DA_HEAD_DIM),
            v_all.reshape(depth, b, s, DA_HEADS, DA_V_DIM),
            jnp.stack(outs["ks"]), jnp.stack(outs["vs"]), jnp.stack(outs["gv"]))
```

```python
import functools
import math

import jax
import jax.numpy as jnp
from jax import lax
from jax.experimental import pallas as pl
from jax.experimental.pallas import tpu as pltpu

F32 = jnp.float32
BF16 = jnp.bfloat16

DA_HEADS = 4
DA_HEAD_DIM = 64
DA_V_DIM = 2 * DA_HEAD_DIM
QK_COLS = DA_HEADS * 2 * DA_HEAD_DIM
DA_WIDTH = DA_HEADS * DA_V_DIM
ROT_DIM = DA_HEAD_DIM // 4
ROPE_THETA = 500000.0
GM_GROUPS = 4
GM_CH = 128
GM_WIDTH = GM_GROUPS * GM_CH
CHUNK = 128
PAGE_SIZE = 128
TOP_K = 2
RMS_EPS = 1e-6
SUBLN_EPS = 1e-5
LN_EPS = 1e-5
NEG_BIG = -1e30
LOG2_E = math.log2(math.e)
LANES = 128

VMEM_LIMIT = 52 * 1024 * 1024


def _cparams(sem):
    return pltpu.CompilerParams(dimension_semantics=sem, vmem_limit_bytes=VMEM_LIMIT)


def _nt_dot(a, b):
    return lax.dot_general(a, b, (((1,), (1,)), ((), ())), preferred_element_type=F32)


def _rmsnorm_rows(x, g, eps):
    return x * lax.rsqrt(jnp.mean(x * x, axis=-1, keepdims=True) + eps) * g


def _tiles_per_row(d):
    return d // LANES


def _put_rows(ref, x):
    n, d = x.shape
    if ref.shape == x.shape:
        ref[...] = x
        return
    tpr = _tiles_per_row(d)
    assert ref.shape == (n * tpr, LANES)
    for c in range(tpr):
        ref[pl.ds(c, n, stride=tpr), :] = x[:, c * LANES:(c + 1) * LANES]


def _get_rows(ref, n, d):
    tpr = _tiles_per_row(d)
    return jnp.concatenate([ref[pl.ds(c, n, stride=tpr), :] for c in range(tpr)], axis=1)


def _gelu_exact(x):
    return 0.5 * x * (1.0 + lax.erf(x * (2.0 ** -0.5)))


def _diff_lambda(lq1, lk1, lq2, lk2, lam_init):
    a = jnp.exp(jnp.sum(lq1 * lk1, axis=-1, keepdims=True))
    b = jnp.exp(jnp.sum(lq2 * lk2, axis=-1, keepdims=True))
    return a - b + lam_init


def _put_head(ref, t, x):
    if ref.shape[-1] == DA_HEADS * LANES:
        ref[:, t * LANES:(t + 1) * LANES] = x
    else:
        for j in range(ref.shape[0]):
            ref.at[j][pl.ds(t, x.shape[0], stride=DA_HEADS), :] = x


def _proj_body(n_alias, h_ref, g_ref, w_ref, cos_ref, sa_ref, sb_ref, lng_ref, lnb_ref, *rest):
    q_ref, k_ref, v_ref, kb_ref, vb_ref, u_ref, gv_ref = rest[n_alias:]
    h = h_ref[...]
    xn = _rmsnorm_rows(h, g_ref[...], RMS_EPS).astype(BF16)
    z = jnp.dot(xn, w_ref[...], preferred_element_type=F32)
    cos, sa, sb = cos_ref[...], sa_ref[...], sb_ref[...]

    def rope(x):
        return (x * cos + pltpu.roll(x, ROT_DIM // 2, 1) * sa
                + pltpu.roll(x, LANES - ROT_DIM // 2, 1) * sb)

    scale = DA_HEAD_DIM ** -0.5 * LOG2_E
    ones = jnp.ones((h.shape[0], DA_V_DIM), vb_ref.dtype)
    for t in range(DA_HEADS):
        sl = slice(t * LANES, (t + 1) * LANES)
        q_ref[:, sl] = (rope(z[:, sl]) * scale).astype(q_ref.dtype)
        kt = rope(z[:, QK_COLS + t * LANES:QK_COLS + (t + 1) * LANES])
        _put_head(k_ref, t, kt)
        kb_ref[:, sl] = kt.astype(kb_ref.dtype)
        vt = z[:, 2 * QK_COLS + t * DA_V_DIM:2 * QK_COLS + (t + 1) * DA_V_DIM]
        _put_head(v_ref, t, vt)
        vb_ref[:, 2 * t * DA_V_DIM:(2 * t + 1) * DA_V_DIM] = vt.astype(vb_ref.dtype)
        vb_ref[:, (2 * t + 1) * DA_V_DIM:(2 * t + 2) * DA_V_DIM] = ones
    gz = _gelu_exact(z[:, 2 * QK_COLS + DA_WIDTH:])
    u_ref[...] = gz[:, :GM_WIDTH].astype(u_ref.dtype)
    gvr = gz[:, GM_WIDTH:]
    mu = jnp.mean(gvr, axis=-1, keepdims=True)
    d = gvr - mu
    var = jnp.mean(d * d, axis=-1, keepdims=True)
    gv = d * lax.rsqrt(var + LN_EPS) * lng_ref[...] + lnb_ref[...]
    gv_ref[...] = gv.astype(gv_ref.dtype)


def _proj(h, g, w_bf, tables, lng, lnb, tm, u_dtype, gv_dtype, kv_stack=None):
    m, d = h.shape
    cos_t, sa_t, sb_t = tables
    ntab = cos_t.shape[0] // tm
    row = lambda i: (i, 0)
    fixed = lambda i: (0, 0)
    tab = lambda i: (i % ntab, 0)
    wide = lambda dt: jax.ShapeDtypeStruct((m, QK_COLS), dt)
    in_specs = [
        pl.BlockSpec((tm, d), row),
        pl.BlockSpec((1, d), fixed),
        pl.BlockSpec(w_bf.shape, fixed),
        pl.BlockSpec((tm, LANES), tab),
        pl.BlockSpec((tm, LANES), tab),
        pl.BlockSpec((tm, LANES), tab),
        pl.BlockSpec((1, GM_WIDTH), fixed),
        pl.BlockSpec((1, GM_WIDTH), fixed),
    ]
    args = [h, g, w_bf, cos_t, sa_t, sb_t, lng, lnb]
    kv_spec = pl.BlockSpec((tm, QK_COLS), row)
    kv_shape = wide(F32)
    aliases = {}
    if kv_stack is not None:
        layer, depth, k_all, v_all = kv_stack
        kv_shape = jax.ShapeDtypeStruct((depth, m * DA_HEADS, LANES), F32)
        if k_all is None:
            assert layer == 0
            kv_spec = pl.BlockSpec((depth, tm * DA_HEADS, LANES), lambda i: (0, i, 0))
        else:
            kv_spec = pl.BlockSpec((1, tm * DA_HEADS, LANES), lambda i: (layer, i, 0))
            aliases = {len(args): 1, len(args) + 1: 2}
            in_specs += [pl.BlockSpec(memory_space=pl.ANY)] * 2
            args += [k_all, v_all]
    return pl.pallas_call(
        functools.partial(_proj_body, len(aliases)),
        grid=(m // tm,),
        in_specs=in_specs,
        out_specs=[pl.BlockSpec((tm, QK_COLS), row), kv_spec, kv_spec,
                   pl.BlockSpec((tm, QK_COLS), row), pl.BlockSpec((tm, 2 * DA_WIDTH), row),
                   pl.BlockSpec((tm, GM_WIDTH), row), pl.BlockSpec((tm, GM_WIDTH), row)],
        out_shape=[wide(BF16), kv_shape, kv_shape, wide(BF16),
                   jax.ShapeDtypeStruct((m, 2 * DA_WIDTH), BF16), wide(u_dtype), wide(gv_dtype)],
        input_output_aliases=aliases,
        compiler_params=_cparams(("parallel",)),
        name="proj",
    )(*args)


def _rope_tables(pos):
    inv = ROPE_THETA ** (-jnp.arange(0, ROT_DIM, 2, dtype=F32) / ROT_DIM)
    ang = pos[:, None] * inv[None, :]
    c, s = jnp.cos(ang), jnp.sin(ang)
    j = jnp.arange(LANES) % DA_HEAD_DIM
    first = j < ROT_DIM // 2
    second = (j >= ROT_DIM // 2) & (j < ROT_DIM)
    idx = j % (ROT_DIM // 2)
    cg, sg = c[:, idx], s[:, idx]
    cos_t = jnp.where(first | second, cg, 1.0)
    sa_t = jnp.where(second, sg, 0.0)
    sb_t = jnp.where(first, -sg, 0.0)
    return cos_t, sa_t, sb_t


def _attn_prompt_body(lam_init, blk, lq1_ref, lk1_ref, lq2_ref, lk2_ref, sg_ref,
                      q_ref, k_ref, v_ref, o_ref, s_a, s_b, mx_a, mx_b, acc_sc):
    n_q = q_ref.shape[1] // blk
    buf_a, buf_b = (s_a, mx_a), (s_b, mx_b)
    lane = lax.broadcasted_iota(jnp.int32, (blk, LANES), 1)
    r2 = lax.broadcasted_iota(jnp.int32, (2 * blk, blk), 0)
    c2 = lax.broadcasted_iota(jnp.int32, (2 * blk, blk), 1)
    causal = c2 <= jnp.where(r2 >= blk, r2 - blk, r2)
    lam = _diff_lambda(lq1_ref[...], lk1_ref[...], lq2_ref[...], lk2_ref[...], lam_init)
    gain = sg_ref[...] * (1.0 - lam_init)

    def fold(s):
        out = s[:, :LANES]
        for i in range(1, blk // LANES):
            out = jnp.maximum(out, s[:, i * LANES:(i + 1) * LANES])
        return out

    def stacked_q(qi):
        q = q_ref[0, pl.ds(pl.multiple_of(qi * blk, blk), blk), :]
        zero = jnp.zeros_like(q)
        return jnp.concatenate([jnp.where(lane < DA_HEAD_DIM, q, zero),
                                jnp.where(lane >= DA_HEAD_DIM, q, zero)], axis=0)

    def score_tile(q_st, buf, j):
        s_ref, mx_ref = buf
        s = _nt_dot(q_st, k_ref[0, pl.ds(pl.multiple_of(j * blk, blk), blk), :])
        s_ref[j] = s
        mx_ref[...] = jnp.maximum(mx_ref[...], fold(s))

    def diag_tile(qi, buf):
        s_ref, mx_ref = buf
        q0 = pl.multiple_of(qi * blk, blk)
        s = jnp.where(causal, _nt_dot(stacked_q(qi), k_ref[0, pl.ds(q0, blk), :]), NEG_BIG)
        s_ref[qi] = s
        m = jnp.max(jnp.maximum(mx_ref[...], fold(s)), axis=1, keepdims=True)
        mx_ref[...] = jnp.broadcast_to(m, (2 * blk, LANES))

    def prob_tile(buf, j):
        s_ref, mx_ref = buf
        mb = mx_ref[...]
        p = jnp.exp2(s_ref[j] - jnp.concatenate([mb] * (blk // LANES), axis=1)).astype(BF16)
        acc_sc[...] += jnp.dot(p, v_ref[0, pl.ds(pl.multiple_of(j * blk, blk), blk), :],
                               preferred_element_type=F32)

    def finish(qi):
        acc = acc_sc[...]
        o = (acc[:blk, :DA_V_DIM] / acc[:blk, DA_V_DIM:]
             - lam * (acc[blk:, :DA_V_DIM] / acc[blk:, DA_V_DIM:]))
        o = _rmsnorm_rows(o, gain, SUBLN_EPS)
        o_ref[0, pl.ds(pl.multiple_of(qi * blk, blk), blk), :] = o.astype(o_ref.dtype)

    neg = jnp.full((2 * blk, LANES), NEG_BIG, F32)

    def fused_block(qi, cur, nxt):
        q_next = stacked_q(qi + 1)
        nxt[1][...] = neg
        acc_sc[...] = jnp.zeros(acc_sc.shape, F32)

        def tile(j):
            prob_tile(cur, j)
            score_tile(q_next, nxt, j)

        def two_tiles(j2, c):
            tile(2 * j2)
            tile(2 * j2 + 1)
            return c

        n_t = qi + 1
        lax.fori_loop(0, n_t // 2, two_tiles, 0)

        @pl.when(n_t % 2 == 1)
        def _():
            tile(n_t - 1)

        finish(qi)
        diag_tile(qi + 1, nxt)

    mx_a[...] = neg
    diag_tile(0, buf_a)

    def pair(i2, carry):
        fused_block(2 * i2, buf_a, buf_b)
        fused_block(2 * i2 + 1, buf_b, buf_a)
        return carry

    n_fused = n_q - 1
    lax.fori_loop(0, n_fused // 2, pair, 0)
    if n_fused % 2:
        fused_block(n_fused - 1, buf_a, buf_b)
    last = buf_b if n_fused % 2 else buf_a
    acc_sc[...] = jnp.zeros(acc_sc.shape, F32)

    def tail(j2, c):
        prob_tile(last, 2 * j2)
        prob_tile(last, 2 * j2 + 1)
        return c

    lax.fori_loop(0, n_q // 2, tail, 0)
    if n_q % 2:
        prob_tile(last, n_q - 1)
    finish(n_q - 1)


def _attn_prompt(q, kb, vb1, lams, sg, lam_init, blk):
    b, s, _ = q.shape
    small = pl.BlockSpec((1, DA_HEAD_DIM), lambda bi, h: (0, 0))
    per_head = pl.BlockSpec((1, s, LANES), lambda bi, h: (bi, 0, h))
    return pl.pallas_call(
        functools.partial(_attn_prompt_body, lam_init, blk),
        grid=(b, DA_HEADS),
        in_specs=[small, small, small, small,
                  pl.BlockSpec((1, DA_V_DIM), lambda bi, h: (0, 0)),
                  per_head, per_head,
                  pl.BlockSpec((1, s, 2 * DA_V_DIM), lambda bi, h: (bi, 0, h))],
        out_specs=per_head,
        out_shape=jax.ShapeDtypeStruct((b, s, DA_WIDTH), BF16),
        scratch_shapes=[pltpu.VMEM((s // blk, 2 * blk, blk), F32),
                        pltpu.VMEM((s // blk, 2 * blk, blk), F32),
                        pltpu.VMEM((2 * blk, LANES), F32),
                        pltpu.VMEM((2 * blk, LANES), F32),
                        pltpu.VMEM((2 * blk, 2 * DA_V_DIM), F32)],
        compiler_params=_cparams(("parallel", "parallel")),
        name="attn_prompt",
    )(*lams, sg, q, kb, vb1)


def _attn_sample_body(lam_init, n_tok, pages_per_step, pt_ref, lq1_ref, lk1_ref, lq2_ref, lk2_ref,
                      sg_ref, q_ref, kn_ref, vn_ref, *rest):
    kp = rest[:pages_per_step]
    vp = rest[pages_per_step:2 * pages_per_step]
    o_ref = rest[2 * pages_per_step]
    m_sc, l_sc, a_sc = rest[2 * pages_per_step + 1:]
    j = pl.program_id(1)
    nrow = DA_HEADS * 2 * n_tok
    ncol = PAGE_SIZE * DA_HEADS

    @pl.when(j == 0)
    def _():
        m_sc[...] = jnp.full(m_sc.shape, NEG_BIG, F32)
        l_sc[...] = jnp.zeros(l_sc.shape, F32)
        a_sc[...] = jnp.zeros(a_sc.shape, F32)

    q = q_ref[0]
    row = lax.broadcasted_iota(jnp.int32, (nrow, ncol), 0)
    colk = lax.broadcasted_iota(jnp.int32, (nrow, ncol), 1)
    same_head = (colk % DA_HEADS) == (row // (2 * n_tok))

    def update(s_list, v_list):
        m = m_sc[...]
        mn = m
        for s in s_list:
            mn = jnp.maximum(mn, jnp.max(s, axis=-1, keepdims=True))
        alpha = jnp.exp2(m - mn)
        l = alpha * l_sc[...]
        a = alpha * a_sc[...]
        for s, v in zip(s_list, v_list):
            p = jnp.exp2(s - mn)
            l = l + jnp.sum(p, axis=-1, keepdims=True)
            a = a + jnp.dot(p.astype(BF16), v, preferred_element_type=F32)
        m_sc[...] = mn
        l_sc[...] = l
        a_sc[...] = a

    s_list, v_list = [], []
    for i in range(pages_per_step):
        kb = kp[i][0, 0].astype(BF16)
        s_list.append(jnp.where(same_head, _nt_dot(q, kb), NEG_BIG))
        v_list.append(vp[i][0, 0].astype(BF16))
    update(s_list, v_list)

    @pl.when(j == pl.num_programs(1) - 1)
    def _():
        rown = lax.broadcasted_iota(jnp.int32, (nrow, kn_ref.shape[1]), 0)
        coln = lax.broadcasted_iota(jnp.int32, (nrow, kn_ref.shape[1]), 1)
        ok = ((coln < n_tok * DA_HEADS) & ((coln % DA_HEADS) == (rown // (2 * n_tok)))
              & ((coln // DA_HEADS) <= (rown % n_tok)))
        s = jnp.where(ok, _nt_dot(q, kn_ref[0]), NEG_BIG)
        update([s], [vn_ref[0]])
        a = a_sc[...] / l_sc[...]
        lam = _diff_lambda(lq1_ref[...], lk1_ref[...], lq2_ref[...], lk2_ref[...], lam_init)
        for h in range(DA_HEADS):
            base = h * 2 * n_tok
            o = a[base:base + n_tok] - lam * a[base + n_tok:base + 2 * n_tok]
            o = _rmsnorm_rows(o, sg_ref[...], SUBLN_EPS) * (1.0 - lam_init)
            o_ref[0, :, h * LANES:(h + 1) * LANES] = o


def _attn_sample(layer, q_rows, kn, vn, cache_k, cache_v, page_table, lams, sg, lam_init,
                 n_tok, pages_per_step):
    db, nrow, _ = q_rows.shape
    n_pages = page_table.shape[1]
    assert n_pages % pages_per_step == 0
    pt = page_table.reshape(-1)
    small = pl.BlockSpec((1, DA_HEAD_DIM), lambda b, j, p: (0, 0))
    per_seq = lambda b, j, p: (b, 0, 0)

    def page_spec(i):
        return pl.BlockSpec(
            (1, 1, PAGE_SIZE * DA_HEADS, LANES),
            lambda b, j, p: (layer, p[b * n_pages + j * pages_per_step + i], 0, 0))

    pages = [page_spec(i) for i in range(pages_per_step)]
    grid_spec = pltpu.PrefetchScalarGridSpec(
        num_scalar_prefetch=1,
        grid=(db, n_pages // pages_per_step),
        in_specs=[small, small, small, small,
                  pl.BlockSpec((1, DA_V_DIM), lambda b, j, p: (0, 0)),
                  pl.BlockSpec((1, nrow, LANES), per_seq),
                  pl.BlockSpec((1,) + kn.shape[1:], per_seq),
                  pl.BlockSpec((1,) + vn.shape[1:], per_seq)] + pages + pages,
        out_specs=pl.BlockSpec((1, n_tok, DA_WIDTH), per_seq),
        scratch_shapes=[pltpu.VMEM((nrow, 1), F32), pltpu.VMEM((nrow, 1), F32),
                        pltpu.VMEM((nrow, DA_V_DIM), F32)],
    )
    return pl.pallas_call(
        functools.partial(_attn_sample_body, lam_init, n_tok, pages_per_step),
        grid_spec=grid_spec,
        out_shape=jax.ShapeDtypeStruct((db, n_tok, DA_WIDTH), F32),
        compiler_params=_cparams(("parallel", "arbitrary")),
        name="attn_sample",
    )(pt, *lams, sg, q_rows, kn, vn, *([cache_k] * pages_per_step), *([cache_v] * pages_per_step))


def _mix_out_prompt_body(n_chunk, a_ref, u_ref, gv_ref, h_ref, ws_ref, bst_ref, wo_ref, gf_ref,
                         ho_ref, xn_ref, g_sc):
    r = lax.broadcasted_iota(jnp.int32, (CHUNK, CHUNK), 0)
    c = lax.broadcasted_iota(jnp.int32, (CHUNK, CHUNK), 1)
    for g in range(GM_GROUPS):
        w = jnp.where(c <= r, ws_ref[g], 0.0).astype(BF16)
        bias = bst_ref[:, g:g + 1]
        lanes = slice(g * GM_CH, (g + 1) * GM_CH)
        for ci in range(n_chunk):
            rows = slice(ci * CHUNK, (ci + 1) * CHUNK)
            m = jnp.dot(w, gv_ref[rows, lanes], preferred_element_type=F32) + bias
            g_sc[rows, lanes] = (u_ref[rows, lanes].astype(F32) * m).astype(BF16)
    out = (jnp.dot(a_ref[...], wo_ref[:DA_WIDTH, :], preferred_element_type=F32)
           + jnp.dot(g_sc[...], wo_ref[DA_WIDTH:, :], preferred_element_type=F32))
    hn = h_ref[...] + out
    ho_ref[...] = hn
    _put_rows(xn_ref, _rmsnorm_rows(hn, gf_ref[...], RMS_EPS))


def _mix_out_sample_body(n_tok, n_seq, a_ref, u_ref, gv_ref, h_ref, ws_ref, bs_ref, wo_ref, gf_ref,
                         ho_ref, xn_ref, g_sc):
    for g in range(GM_GROUPS):
        lanes = slice(g * GM_CH, (g + 1) * GM_CH)
        for t in range(n_tok):
            m = jnp.zeros((n_seq, GM_CH), F32) + bs_ref[g:g + 1, t:t + 1]
            for s in range(t + 1):
                m = m + ws_ref[g, t:t + 1, s:s + 1] * gv_ref[s * n_seq:(s + 1) * n_seq, lanes]
            rows = slice(t * n_seq, (t + 1) * n_seq)
            g_sc[rows, lanes] = u_ref[rows, lanes] * m
    out = (jnp.dot(a_ref[...], wo_ref[:DA_WIDTH, :], preferred_element_type=F32)
           + jnp.dot(g_sc[...].astype(BF16), wo_ref[DA_WIDTH:, :], preferred_element_type=F32))
    hn = h_ref[...] + out
    ho_ref[...] = hn
    _put_rows(xn_ref, _rmsnorm_rows(hn, gf_ref[...], RMS_EPS))


def _mix_out(a, u, gv, h, ws, bs, wo_bf, gf, tm, sample_shape=None, tile_rows=False):
    m, d = h.shape
    row = lambda i: (i, 0)
    tpr = _tiles_per_row(d) if tile_rows else 1
    fixed2 = lambda i: (0, 0)
    fixed3 = lambda i: (0, 0, 0)
    if sample_shape is None:
        body = functools.partial(_mix_out_prompt_body, tm // CHUNK)
        bias = bs.T
        g_dtype = BF16
    else:
        body = functools.partial(_mix_out_sample_body, *sample_shape)
        bias = bs
        g_dtype = F32
    return pl.pallas_call(
        body,
        grid=(m // tm,),
        in_specs=[pl.BlockSpec((tm, DA_WIDTH), row),
                  pl.BlockSpec((tm, GM_WIDTH), row),
                  pl.BlockSpec((tm, GM_WIDTH), row),
                  pl.BlockSpec((tm, d), row),
                  pl.BlockSpec(ws.shape, fixed3),
                  pl.BlockSpec(bias.shape, fixed2),
                  pl.BlockSpec(wo_bf.shape, fixed2),
                  pl.BlockSpec((1, d), fixed2)],
        out_specs=[pl.BlockSpec((tm, d), row), pl.BlockSpec((tm * tpr, d // tpr), row)],
        out_shape=[jax.ShapeDtypeStruct((m, d), F32),
                   jax.ShapeDtypeStruct((m * tpr, d // tpr), F32)],
        scratch_shapes=[pltpu.VMEM((tm, GM_WIDTH), g_dtype)],
        compiler_params=_cparams(("parallel",)),
        name="mix_out",
    )(a, u, gv, h, ws, bias, wo_bf, gf)


def _ffn_body(xn_ref, h_ref, wg_ref, wu_ref, wd_ref, o_ref):
    x = xn_ref[...].astype(BF16)
    g = jnp.dot(x, wg_ref[...], preferred_element_type=F32)
    u = jnp.dot(x, wu_ref[...], preferred_element_type=F32)
    hid = (g * jax.nn.sigmoid(g) * u).astype(BF16)
    o_ref[...] = h_ref[...] + jnp.dot(hid, wd_ref[...], preferred_element_type=F32)


def _ffn(xn, h, wg, wu, wd, tm):
    m, d = h.shape
    ff = wg.shape[1]
    resident = dict(pipeline_mode=pl.Buffered(1))
    return pl.pallas_call(
        _ffn_body,
        grid=(m // tm,),
        in_specs=[pl.BlockSpec((tm, d), lambda i: (i, 0)),
                  pl.BlockSpec((tm, d), lambda i: (i, 0)),
                  pl.BlockSpec((d, ff), lambda i: (0, 0), **resident),
                  pl.BlockSpec((d, ff), lambda i: (0, 0), **resident),
                  pl.BlockSpec((ff, d), lambda i: (0, 0), **resident)],
        out_specs=pl.BlockSpec((tm, d), lambda i: (i, 0)),
        out_shape=jax.ShapeDtypeStruct((m, d), F32),
        compiler_params=_cparams(("parallel",)),
        name="ffn",
    )(xn, h, wg, wu, wd)


R_E1, R_E2, R_G1, R_G2, R_RANK1, R_RANK2 = range(6)


def _router_body(n_exp, xn_ref, wr_ref, base_ref, o_ref, cnt_ref, cnt_sc):
    i = pl.program_id(0)

    @pl.when(i == 0)
    def _():
        cnt_sc[...] = base_ref[...]

    d = wr_ref.shape[0]
    tm = xn_ref.shape[0] // _tiles_per_row(d)
    logits = jnp.dot(_get_rows(xn_ref, tm, d), wr_ref[...], preferred_element_type=F32,
                     precision=lax.Precision.HIGHEST)
    col = lax.broadcasted_iota(jnp.int32, logits.shape, 1)
    logits = jnp.where(col < n_exp, logits, -jnp.inf)
    big = jnp.int32(LANES)
    v1 = jnp.max(logits, axis=-1, keepdims=True)
    i1 = jnp.min(jnp.where(logits == v1, col, big), axis=-1, keepdims=True)
    rest = jnp.where(col == i1, -jnp.inf, logits)
    v2 = jnp.max(rest, axis=-1, keepdims=True)
    i2 = jnp.min(jnp.where(rest == v2, col, big), axis=-1, keepdims=True)
    e = jnp.exp(v2 - v1)
    g1 = 1.0 / (1.0 + e)
    g2 = e / (1.0 + e)
    hit = (col == i1) | (col == i2)
    r = lax.broadcasted_iota(jnp.int32, (tm, tm), 0)
    c = lax.broadcasted_iota(jnp.int32, (tm, tm), 1)
    below = jnp.where(c < r, 1.0, 0.0).astype(BF16)
    prefix = jnp.dot(below, jnp.where(hit, 1.0, 0.0).astype(BF16), preferred_element_type=F32)
    rank = prefix + cnt_sc[...]
    r1 = jnp.sum(jnp.where(col == i1, rank, 0.0), axis=-1, keepdims=True)
    r2 = jnp.sum(jnp.where(col == i2, rank, 0.0), axis=-1, keepdims=True)
    cnt_sc[...] += jnp.sum(jnp.where(hit, 1.0, 0.0), axis=0, keepdims=True)
    rec = jnp.zeros(logits.shape, F32)
    for idx, val in ((R_E1, i1.astype(F32)), (R_E2, i2.astype(F32)), (R_G1, g1), (R_G2, g2),
                     (R_RANK1, r1), (R_RANK2, r2)):
        rec = jnp.where(col == idx, val, rec)
    o_ref[...] = rec
    cnt_ref[...] = cnt_sc[...]


def _router(xn_t, wr_pad, base_counts, n_exp, tm):
    tpr = _tiles_per_row(wr_pad.shape[0])
    m = xn_t.shape[0] // tpr
    return pl.pallas_call(
        functools.partial(_router_body, n_exp),
        grid=(m // tm,),
        in_specs=[pl.BlockSpec((tm * tpr, LANES), lambda i: (i, 0)),
                  pl.BlockSpec(wr_pad.shape, lambda i: (0, 0)),
                  pl.BlockSpec((1, LANES), lambda i: (0, 0))],
        out_specs=[pl.BlockSpec((tm, LANES), lambda i: (i, 0)),
                   pl.BlockSpec((1, LANES), lambda i: (0, 0))],
        out_shape=[jax.ShapeDtypeStruct((m, LANES), F32), jax.ShapeDtypeStruct((1, LANES), F32)],
        scratch_shapes=[pltpu.VMEM((1, LANES), F32)],
        compiler_params=_cparams(("arbitrary",)),
        name="router",
    )(xn_t, wr_pad, base_counts)


DMA_UNROLL = 8


def _tile_copy(src, src_row, dst, dst_row, tpr, sem):
    return pltpu.make_async_copy(src.at[pl.ds(pl.multiple_of(src_row * tpr, tpr), tpr), :],
                                 dst.at[pl.ds(pl.multiple_of(dst_row * tpr, tpr), tpr), :], sem)


def _dispatch_body(tm, tpr, slot_ref, xn_ref, xs_in_ref, xs_ref, sem):
    del xs_in_ref

    def start(t, c):
        for k in range(TOP_K):
            _tile_copy(xn_ref, t, xs_ref, slot_ref[0, 0, TOP_K * t + k], tpr, sem).start()
        return c

    def wait(t, c):
        for k in range(TOP_K):
            _tile_copy(xn_ref, t, xs_ref, t, tpr, sem).wait()
        return c

    lax.fori_loop(0, tm, start, 0, unroll=DMA_UNROLL)
    lax.fori_loop(0, tm, wait, 0, unroll=DMA_UNROLL)


def _dispatch(slots, xn_t, xs_t, tm, tpr):
    m = xn_t.shape[0] // tpr
    slots3 = slots.reshape(m // tm, 1, tm * TOP_K)
    return pl.pallas_call(
        functools.partial(_dispatch_body, tm, tpr),
        grid=(m // tm,),
        in_specs=[pl.BlockSpec((1, 1, tm * TOP_K), lambda i: (i, 0, 0), memory_space=pltpu.SMEM),
                  pl.BlockSpec((tm * tpr, LANES), lambda i: (i, 0)),
                  pl.BlockSpec(memory_space=pl.ANY)],
        out_specs=pl.BlockSpec(memory_space=pl.ANY),
        out_shape=jax.ShapeDtypeStruct(xs_t.shape, xs_t.dtype),
        scratch_shapes=[pltpu.SemaphoreType.DMA(())],
        input_output_aliases={2: 0},
        compiler_params=_cparams(("arbitrary",)),
        name="moe_dispatch",
    )(slots3, xn_t, xs_t)


def _combine_body(tm, tpr, slot_cur_ref, slot_nxt_ref, rec_ref, h_ref, g_ref, ys_ref, o_ref,
                  buf, sem):
    i = pl.program_id(0)
    n = pl.num_programs(0)
    d = h_ref.shape[1]

    def fetch(slot_ref, half):
        def start(t, c):
            for k in range(TOP_K):
                _tile_copy(ys_ref, slot_ref[0, 0, TOP_K * t + k], buf.at[half, k], t, tpr,
                           sem.at[half]).start()
            return c
        lax.fori_loop(0, tm, start, 0, unroll=DMA_UNROLL)

    def wait_block(half):
        def wait(t, c):
            for k in range(TOP_K):
                _tile_copy(ys_ref, t, buf.at[half, k], t, tpr, sem.at[half]).wait()
            return c
        lax.fori_loop(0, tm, wait, 0, unroll=DMA_UNROLL)

    def step(half):
        @pl.when(i == 0)
        def _():
            fetch(slot_cur_ref, half)

        @pl.when(i + 1 < n)
        def _():
            fetch(slot_nxt_ref, 1 - half)

        wait_block(half)
        rec = rec_ref[...]
        y = (rec[:, R_G1:R_G1 + 1] * _get_rows(buf.at[half, 0], tm, d)
             + rec[:, R_G2:R_G2 + 1] * _get_rows(buf.at[half, 1], tm, d))
        o_ref[...] = _rmsnorm_rows(h_ref[...] + y, g_ref[...], RMS_EPS)

    @pl.when(i % 2 == 0)
    def _():
        step(0)

    @pl.when(i % 2 == 1)
    def _():
        step(1)


def _combine_norm(slots, rec, h, g, ys_t, tm):
    m, d = h.shape
    tpr = _tiles_per_row(d)
    n = m // tm
    slots3 = slots.reshape(n, 1, tm * TOP_K)
    row = lambda i: (i, 0)
    return pl.pallas_call(
        functools.partial(_combine_body, tm, tpr),
        grid=(n,),
        in_specs=[pl.BlockSpec((1, 1, tm * TOP_K), lambda i: (i, 0, 0), memory_space=pltpu.SMEM),
                  pl.BlockSpec((1, 1, tm * TOP_K), lambda i: (jnp.minimum(i + 1, n - 1), 0, 0),
                               memory_space=pltpu.SMEM),
                  pl.BlockSpec((tm, LANES), row),
                  pl.BlockSpec((tm, d), row),
                  pl.BlockSpec((1, d), lambda i: (0, 0)),
                  pl.BlockSpec(memory_space=pl.ANY)],
        out_specs=pl.BlockSpec((tm, d), row),
        out_shape=jax.ShapeDtypeStruct((m, d), F32),
        scratch_shapes=[pltpu.VMEM((2, TOP_K, tm * tpr, LANES), F32),
                        pltpu.SemaphoreType.DMA((2,))],
        compiler_params=_cparams(("arbitrary",)),
        name="moe_combine_norm",
    )(slots3, slots3, rec, h, g, ys_t)


def _moe_body(tm, be_ref, nv_ref, xs_ref, wg_ref, wu_ref, wd_ref, o_ref, x_sc, acc_sc):
    i = pl.program_id(0)
    f = pl.program_id(1)
    last = pl.num_programs(1) - 1
    live = i < nv_ref[0]
    d = x_sc.shape[1]

    @pl.when(live & (f == 0))
    def _():
        acc_sc[...] = jnp.zeros(acc_sc.shape, F32)
        x_sc[...] = _get_rows(xs_ref, tm, d).astype(BF16)

    @pl.when(live)
    def _():
        x = x_sc[...]
        g = jnp.dot(x, wg_ref[0], preferred_element_type=F32)
        u = jnp.dot(x, wu_ref[0], preferred_element_type=F32)
        hid = (g * jax.nn.sigmoid(g) * u).astype(BF16)
        acc_sc[...] += jnp.dot(hid, wd_ref[0], preferred_element_type=F32)

    @pl.when(live & (f == last))
    def _():
        _put_rows(o_ref, acc_sc[...])

    @pl.when(jnp.logical_not(live) & (f == last))
    def _():
        o_ref[...] = jnp.zeros(o_ref.shape, F32)


def _moe_experts(xs_t, block_expert, n_valid, wg, wu, wd, tm, tf):
    d, ff = wg.shape[1], wg.shape[2]
    tpr = _tiles_per_row(d)
    p = xs_t.shape[0] // tpr
    rows = pl.BlockSpec((tm * tpr, LANES), lambda i, f, be, nv: (i, 0))
    grid_spec = pltpu.PrefetchScalarGridSpec(
        num_scalar_prefetch=2,
        grid=(p // tm, ff // tf),
        in_specs=[rows,
                  pl.BlockSpec((1, d, tf), lambda i, f, be, nv: (be[i], 0, f)),
                  pl.BlockSpec((1, d, tf), lambda i, f, be, nv: (be[i], 0, f)),
                  pl.BlockSpec((1, tf, d), lambda i, f, be, nv: (be[i], f, 0))],
        out_specs=rows,
        scratch_shapes=[pltpu.VMEM((tm, d), BF16), pltpu.VMEM((tm, d), F32)],
    )
    return pl.pallas_call(
        functools.partial(_moe_body, tm),
        grid_spec=grid_spec,
        out_shape=jax.ShapeDtypeStruct(xs_t.shape, F32),
        compiler_params=_cparams(("parallel", "arbitrary")),
        name="moe_experts",
    )(block_expert, n_valid, xs_t, wg, wu, wd)


def _moe_final(groups, w_router, wg, wu, wd, final_g, tm, tf):
    d = w_router.shape[0]
    n_exp = w_router.shape[1]
    n_tok = sum(h.shape[0] for _, h, _ in groups)
    wr_pad = jnp.zeros((d, LANES), F32).at[:, :n_exp].set(w_router)
    counts = jnp.zeros((1, LANES), F32)
    recs = []
    for xn, h, _ in groups:
        rec, counts = _router(xn, wr_pad, counts, n_exp, _pick(h.shape[0], 512))
        recs.append(rec)
    cnt = counts[0, :n_exp].astype(jnp.int32)
    padded = ((cnt + tm - 1) // tm) * tm
    group_end = jnp.cumsum(padded)
    group_start = group_end - padded
    n_slots = ((n_tok * TOP_K + n_exp * (tm - 1)) // tm) * tm
    n_blocks = n_slots // tm
    block_start = jnp.arange(n_blocks, dtype=jnp.int32) * tm
    n_valid = (group_end[-1] // tm).astype(jnp.int32)
    block_expert = jnp.sum((block_start[:, None] >= group_end[None, :]).astype(jnp.int32), axis=1)
    last_valid = jnp.sum((jnp.maximum(n_valid - 1, 0) * tm >= group_end).astype(jnp.int32))
    block_expert = jnp.minimum(jnp.where(block_start < n_valid * tm, block_expert, last_valid),
                               n_exp - 1).astype(jnp.int32)

    def slots_of(rec):
        e = rec[:, R_E1:R_E2 + 1].astype(jnp.int32)
        rank = rec[:, R_RANK1:R_RANK2 + 1].astype(jnp.int32)
        start = jnp.sum(jnp.where(e[..., None] == jnp.arange(n_exp), group_start, 0), axis=-1)
        return (start + rank).astype(jnp.int32)

    slots = [slots_of(rec) for rec in recs]
    tpr = _tiles_per_row(d)
    xs = jnp.zeros((n_slots * tpr, LANES), F32)
    for (xn, _, tb), sl in zip(groups, slots):
        xs = _dispatch(sl, xn, xs, tb, tpr)
    ys = _moe_experts(xs, block_expert, n_valid.reshape(1), wg, wu, wd, tm, tf)
    return [_combine_norm(sl, rec, h, final_g, ys, tb)
            for (_, h, tb), sl, rec in zip(groups, slots, recs)]


def _pick(total, pref):
    t = min(total, pref)
    while total % t:
        t //= 2
    return t


def kernel(x_prompt, x_sample, cache_k, cache_v, page_table, w_in, w_out, norm_mix_g, norm_ffn_g,
           lam_q1, lam_k1, lam_q2, lam_k2, subln_g, gm_ln_g, gm_ln_b, gm_ws, gm_bs,
           ffn_wg, ffn_wu, ffn_wd, moe_router, moe_wg, moe_wu, moe_wd, final_norm_g):
    b, s, d = x_prompt.shape
    db, t = x_sample.shape[:2]
    depth = w_in.shape[0]
    n_pages = page_table.shape[1]
    past = n_pages * PAGE_SIZE
    n_p, n_s = b * s, db * t
    assert depth == 2 and s % CHUNK == 0

    tm_p = _pick(s, 512)
    blk = _pick(s, 512)
    pages_per_step = _pick(n_pages, 16)

    hp = x_prompt.reshape(n_p, d)
    hs = x_sample.transpose(1, 0, 2).reshape(n_s, d)
    tab_p = _rope_tables(jnp.arange(s, dtype=F32))
    pos_s = jnp.arange(t, dtype=F32) + jnp.float32(past)
    tab_s = _rope_tables(jnp.repeat(pos_s, db))
    ck = cache_k.reshape(cache_k.shape[0], cache_k.shape[1], PAGE_SIZE * DA_HEADS, LANES)
    cv = cache_v.reshape(cache_v.shape[0], cache_v.shape[1], PAGE_SIZE * DA_HEADS, LANES)
    lane = jnp.arange(LANES)
    comp_mask = jnp.stack([lane < DA_HEAD_DIM, lane >= DA_HEAD_DIM])
    new_rows = PAGE_SIZE
    assert t * DA_HEADS <= new_rows

    outs = {k: [] for k in ("ks", "vs", "gv")}
    y_p = y_s = k_all = v_all = None
    for l in range(depth):
        lam_init = 0.8 - 0.6 * math.exp(-0.3 * l)
        w_in_bf = w_in[l].astype(BF16)
        w_out_bf = w_out[l].astype(BF16)
        g_mix = norm_mix_g[l][None]
        g_ffn = norm_ffn_g[l][None]
        lng, lnb = gm_ln_g[l][None], gm_ln_b[l][None]
        lams = (lam_q1[l][None], lam_k1[l][None], lam_q2[l][None], lam_k2[l][None])
        sg = subln_g[l][None]

        q, k_all, v_all, kb, vb, u, gv = _proj(hp, g_mix, w_in_bf, tab_p, lng, lnb, tm_p, BF16, BF16,
                                               kv_stack=(l, depth, k_all, v_all))
        a = _attn_prompt(q.reshape(b, s, QK_COLS), kb.reshape(b, s, QK_COLS),
                         vb.reshape(b, s, 2 * DA_WIDTH), lams, sg, lam_init, blk)
        moe_layer = l % 2 == 1
        hp, xn_p = _mix_out(a.reshape(n_p, DA_WIDTH), u, gv, hp, gm_ws[l], gm_bs[l], w_out_bf, g_ffn,
                            tm_p, tile_rows=moe_layer)

        sq, sk, sv, skb, svb, su, sgv = _proj(hs, g_mix, w_in_bf, tab_s, lng, lnb, n_s, F32, F32)
        q5 = sq.reshape(t, db, DA_HEADS, 1, LANES).transpose(1, 2, 3, 0, 4)
        q_rows = jnp.where(comp_mask[None, None, :, None, :], q5, jnp.zeros((), BF16))
        q_rows = q_rows.reshape(db, DA_HEADS * 2 * t, LANES)
        pad = ((0, 0), (0, new_rows - t * DA_HEADS), (0, 0))
        kn = jnp.pad(skb.reshape(t, db, DA_HEADS, LANES).transpose(1, 0, 2, 3).reshape(db, t * DA_HEADS, LANES), pad)
        svb = svb.reshape(t, db, DA_HEADS, 2 * DA_V_DIM)[..., :DA_V_DIM]
        vn = jnp.pad(svb.transpose(1, 0, 2, 3).reshape(db, t * DA_HEADS, LANES), pad)
        sa = _attn_sample(l, q_rows, kn, vn, ck, cv, page_table, lams, sg, lam_init, t, pages_per_step)
        sa = sa.transpose(1, 0, 2).reshape(n_s, DA_WIDTH).astype(BF16)
        hs, xn_s = _mix_out(sa, su, sgv, hs, gm_ws[l], gm_bs[l], w_out_bf, g_ffn, n_s,
                            sample_shape=(t, db), tile_rows=moe_layer)

        outs["ks"].append(sk.reshape(t, db, DA_HEADS, 2 * DA_HEAD_DIM).transpose(1, 0, 2, 3))
        outs["vs"].append(sv.reshape(t, db, DA_HEADS, DA_V_DIM).transpose(1, 0, 2, 3))
        outs["gv"].append(sgv.reshape(t, db, GM_GROUPS, GM_CH).transpose(1, 0, 2, 3))

        j = l // 2
        if l % 2 == 0:
            wg, wu, wd = ffn_wg[j].astype(BF16), ffn_wu[j].astype(BF16), ffn_wd[j].astype(BF16)
            hp = _ffn(xn_p, hp, wg, wu, wd, tm_p)
            hs = _ffn(xn_s, hs, wg, wu, wd, n_s)
        else:
            wg, wu, wd = moe_wg[j].astype(BF16), moe_wu[j].astype(BF16), moe_wd[j].astype(BF16)
            y_p, y_s = _moe_final([(xn_p, hp, _pick(n_p, 256)), (xn_s, hs, n_s)], moe_router[j],
                                  wg, wu, wd, final_norm_g[None], 512, _pick(wg.shape[2], 1792))

    y_prompt = y_p.reshape(b, s, d)
    y_sample = y_s.reshape(t, db, d).transpose(1, 0, 2)
    return (y_prompt, y_sample,
            k_all.reshape(depth, b, s, DA_HEADS, 2 * DA_HEAD_DIM),
            v_all.reshape(depth, b, s, DA_HEADS, DA_V_DIM),
            jnp.stack(outs["ks"]), jnp.stack(outs["vs"]), jnp.stack(outs["gv"]))
```

```python
import functools
import math

import jax
import jax.numpy as jnp
from jax import lax
from jax.experimental import pallas as pl
from jax.experimental.pallas import tpu as pltpu

F32 = jnp.float32
BF16 = jnp.bfloat16

DA_HEADS = 4
DA_HEAD_DIM = 64
DA_V_DIM = 2 * DA_HEAD_DIM
QK_COLS = DA_HEADS * 2 * DA_HEAD_DIM
DA_WIDTH = DA_HEADS * DA_V_DIM
ROT_DIM = DA_HEAD_DIM // 4
ROPE_THETA = 500000.0
GM_GROUPS = 4
GM_CH = 128
GM_WIDTH = GM_GROUPS * GM_CH
CHUNK = 128
PAGE_SIZE = 128
TOP_K = 2
RMS_EPS = 1e-6
SUBLN_EPS = 1e-5
LN_EPS = 1e-5
NEG_BIG = -1e30
LOG2_E = math.log2(math.e)
LANES = 128

VMEM_LIMIT = 52 * 1024 * 1024


def _cparams(sem):
    return pltpu.CompilerParams(dimension_semantics=sem, vmem_limit_bytes=VMEM_LIMIT)


def _nt_dot(a, b):
    return lax.dot_general(a, b, (((1,), (1,)), ((), ())), preferred_element_type=F32)


def _rmsnorm_rows(x, g, eps):
    return x * lax.rsqrt(jnp.mean(x * x, axis=-1, keepdims=True) + eps) * g


def _tiles_per_row(d):
    return d // LANES


def _put_rows(ref, x):
    n, d = x.shape
    if ref.shape == x.shape:
        ref[...] = x
        return
    tpr = _tiles_per_row(d)
    assert ref.shape == (n * tpr, LANES)
    for c in range(tpr):
        ref[pl.ds(c, n, stride=tpr), :] = x[:, c * LANES:(c + 1) * LANES]


def _get_rows(ref, n, d):
    tpr = _tiles_per_row(d)
    return jnp.concatenate([ref[pl.ds(c, n, stride=tpr), :] for c in range(tpr)], axis=1)


def _gelu_exact(x):
    return 0.5 * x * (1.0 + lax.erf(x * (2.0 ** -0.5)))


def _diff_lambda(lq1, lk1, lq2, lk2, lam_init):
    a = jnp.exp(jnp.sum(lq1 * lk1, axis=-1, keepdims=True))
    b = jnp.exp(jnp.sum(lq2 * lk2, axis=-1, keepdims=True))
    return a - b + lam_init


def _put_head(ref, t, x):
    if ref.shape[-1] == DA_HEADS * LANES:
        ref[:, t * LANES:(t + 1) * LANES] = x
    else:
        for j in range(ref.shape[0]):
            ref.at[j][pl.ds(t, x.shape[0], stride=DA_HEADS), :] = x


def _proj_body(n_alias, h_ref, g_ref, w_ref, cos_ref, sa_ref, sb_ref, lng_ref, lnb_ref, *rest):
    q_ref, k_ref, v_ref, kb_ref, vb_ref, u_ref, gv_ref = rest[n_alias:]
    h = h_ref[...]
    xn = _rmsnorm_rows(h, g_ref[...], RMS_EPS).astype(BF16)
    z = jnp.dot(xn, w_ref[...], preferred_element_type=F32)
    cos, sa, sb = cos_ref[...], sa_ref[...], sb_ref[...]

    def cols(lo, width):
        return z[:, lo:lo + width]

    def rope(x):
        return (x * cos + pltpu.roll(x, ROT_DIM // 2, 1) * sa
                + pltpu.roll(x, LANES - ROT_DIM // 2, 1) * sb)

    scale = DA_HEAD_DIM ** -0.5 * LOG2_E
    ones = jnp.ones((h.shape[0], DA_V_DIM), vb_ref.dtype)
    zq = cols(0, QK_COLS)
    for t in range(DA_HEADS):
        sl = slice(t * LANES, (t + 1) * LANES)
        q_ref[:, sl] = (rope(zq[:, sl]) * scale).astype(q_ref.dtype)
    zk = cols(QK_COLS, QK_COLS)
    for t in range(DA_HEADS):
        sl = slice(t * LANES, (t + 1) * LANES)
        kt = rope(zk[:, sl])
        _put_head(k_ref, t, kt)
        kb_ref[:, sl] = kt.astype(kb_ref.dtype)
    zv = cols(2 * QK_COLS, DA_WIDTH)
    for t in range(DA_HEADS):
        vt = zv[:, t * DA_V_DIM:(t + 1) * DA_V_DIM]
        _put_head(v_ref, t, vt)
        vb_ref[:, 2 * t * DA_V_DIM:(2 * t + 1) * DA_V_DIM] = vt.astype(vb_ref.dtype)
        vb_ref[:, (2 * t + 1) * DA_V_DIM:(2 * t + 2) * DA_V_DIM] = ones
    u_ref[...] = _gelu_exact(cols(2 * QK_COLS + DA_WIDTH, GM_WIDTH)).astype(u_ref.dtype)
    gvr = _gelu_exact(cols(2 * QK_COLS + DA_WIDTH + GM_WIDTH, GM_WIDTH))
    mu = jnp.mean(gvr, axis=-1, keepdims=True)
    d = gvr - mu
    var = jnp.mean(d * d, axis=-1, keepdims=True)
    gv = d * lax.rsqrt(var + LN_EPS) * lng_ref[...] + lnb_ref[...]
    gv_ref[...] = gv.astype(gv_ref.dtype)


def _proj(h, g, w_bf, tables, lng, lnb, tm, u_dtype, gv_dtype, kv_stack=None):
    m, d = h.shape
    cos_t, sa_t, sb_t = tables
    ntab = cos_t.shape[0] // tm
    row = lambda i: (i, 0)
    fixed = lambda i: (0, 0)
    tab = lambda i: (i % ntab, 0)
    wide = lambda dt: jax.ShapeDtypeStruct((m, QK_COLS), dt)
    in_specs = [
        pl.BlockSpec((tm, d), row),
        pl.BlockSpec((1, d), fixed),
        pl.BlockSpec(w_bf.shape, fixed),
        pl.BlockSpec((tm, LANES), tab),
        pl.BlockSpec((tm, LANES), tab),
        pl.BlockSpec((tm, LANES), tab),
        pl.BlockSpec((1, GM_WIDTH), fixed),
        pl.BlockSpec((1, GM_WIDTH), fixed),
    ]
    args = [h, g, w_bf, cos_t, sa_t, sb_t, lng, lnb]
    kv_spec = pl.BlockSpec((tm, QK_COLS), row)
    kv_shape = wide(F32)
    aliases = {}
    if kv_stack is not None:
        layer, depth, k_all, v_all = kv_stack
        kv_shape = jax.ShapeDtypeStruct((depth, m * DA_HEADS, LANES), F32)
        if k_all is None:
            assert layer == 0
            kv_spec = pl.BlockSpec((depth, tm * DA_HEADS, LANES), lambda i: (0, i, 0))
        else:
            kv_spec = pl.BlockSpec((1, tm * DA_HEADS, LANES), lambda i: (layer, i, 0))
            aliases = {len(args): 1, len(args) + 1: 2}
            in_specs += [pl.BlockSpec(memory_space=pl.ANY)] * 2
            args += [k_all, v_all]
    return pl.pallas_call(
        functools.partial(_proj_body, len(aliases)),
        grid=(m // tm,),
        in_specs=in_specs,
        out_specs=[pl.BlockSpec((tm, QK_COLS), row), kv_spec, kv_spec,
                   pl.BlockSpec((tm, QK_COLS), row), pl.BlockSpec((tm, 2 * DA_WIDTH), row),
                   pl.BlockSpec((tm, GM_WIDTH), row), pl.BlockSpec((tm, GM_WIDTH), row)],
        out_shape=[wide(BF16), kv_shape, kv_shape, wide(BF16),
                   jax.ShapeDtypeStruct((m, 2 * DA_WIDTH), BF16), wide(u_dtype), wide(gv_dtype)],
        input_output_aliases=aliases,
        compiler_params=_cparams(("parallel",)),
        name="proj",
    )(*args)


def _rope_tables(pos):
    inv = ROPE_THETA ** (-jnp.arange(0, ROT_DIM, 2, dtype=F32) / ROT_DIM)
    ang = pos[:, None] * inv[None, :]
    c, s = jnp.cos(ang), jnp.sin(ang)
    j = jnp.arange(LANES) % DA_HEAD_DIM
    first = j < ROT_DIM // 2
    second = (j >= ROT_DIM // 2) & (j < ROT_DIM)
    idx = j % (ROT_DIM // 2)
    cg, sg = c[:, idx], s[:, idx]
    cos_t = jnp.where(first | second, cg, 1.0)
    sa_t = jnp.where(second, sg, 0.0)
    sb_t = jnp.where(first, -sg, 0.0)
    return cos_t, sa_t, sb_t


def _attn_prompt_body(lam_init, blk, lq1_ref, lk1_ref, lq2_ref, lk2_ref, sg_ref,
                      q_ref, k_ref, v_ref, o_ref, s_a, s_b, mx_a, mx_b, acc_sc):
    n_q = q_ref.shape[1] // blk
    buf_a, buf_b = (s_a, mx_a), (s_b, mx_b)
    lane = lax.broadcasted_iota(jnp.int32, (blk, LANES), 1)
    r2 = lax.broadcasted_iota(jnp.int32, (2 * blk, blk), 0)
    c2 = lax.broadcasted_iota(jnp.int32, (2 * blk, blk), 1)
    causal = c2 <= jnp.where(r2 >= blk, r2 - blk, r2)
    lam = _diff_lambda(lq1_ref[...], lk1_ref[...], lq2_ref[...], lk2_ref[...], lam_init)
    gain = sg_ref[...] * (1.0 - lam_init)

    def fold(s):
        out = s[:, :LANES]
        for i in range(1, blk // LANES):
            out = jnp.maximum(out, s[:, i * LANES:(i + 1) * LANES])
        return out

    def stacked_q(qi):
        q = q_ref[0, pl.ds(pl.multiple_of(qi * blk, blk), blk), :]
        zero = jnp.zeros_like(q)
        return jnp.concatenate([jnp.where(lane < DA_HEAD_DIM, q, zero),
                                jnp.where(lane >= DA_HEAD_DIM, q, zero)], axis=0)

    def score_tile(q_st, buf, j):
        s_ref, mx_ref = buf
        s = _nt_dot(q_st, k_ref[0, pl.ds(pl.multiple_of(j * blk, blk), blk), :])
        s_ref[j] = s
        mx_ref[...] = jnp.maximum(mx_ref[...], fold(s))

    def diag_tile(qi, buf):
        s_ref, mx_ref = buf
        q0 = pl.multiple_of(qi * blk, blk)
        s = jnp.where(causal, _nt_dot(stacked_q(qi), k_ref[0, pl.ds(q0, blk), :]), NEG_BIG)
        s_ref[qi] = s
        m = jnp.max(jnp.maximum(mx_ref[...], fold(s)), axis=1, keepdims=True)
        mx_ref[...] = jnp.broadcast_to(m, (2 * blk, LANES))

    def prob_tile(buf, j):
        s_ref, mx_ref = buf
        mb = mx_ref[...]
        p = jnp.exp2(s_ref[j] - jnp.concatenate([mb] * (blk // LANES), axis=1)).astype(BF16)
        acc_sc[...] += jnp.dot(p, v_ref[0, pl.ds(pl.multiple_of(j * blk, blk), blk), :],
                               preferred_element_type=F32)

    def finish(qi):
        acc = acc_sc[...]
        o = (acc[:blk, :DA_V_DIM] / acc[:blk, DA_V_DIM:]
             - lam * (acc[blk:, :DA_V_DIM] / acc[blk:, DA_V_DIM:]))
        o = _rmsnorm_rows(o, gain, SUBLN_EPS)
        o_ref[0, pl.ds(pl.multiple_of(qi * blk, blk), blk), :] = o.astype(o_ref.dtype)

    neg = jnp.full((2 * blk, LANES), NEG_BIG, F32)

    def fused_block(qi, cur, nxt):
        q_next = stacked_q(qi + 1)
        nxt[1][...] = neg
        acc_sc[...] = jnp.zeros(acc_sc.shape, F32)

        def tile(j):
            prob_tile(cur, j)
            score_tile(q_next, nxt, j)

        def two_tiles(j2, c):
            tile(2 * j2)
            tile(2 * j2 + 1)
            return c

        n_t = qi + 1
        lax.fori_loop(0, n_t // 2, two_tiles, 0)

        @pl.when(n_t % 2 == 1)
        def _():
            tile(n_t - 1)

        finish(qi)
        diag_tile(qi + 1, nxt)

    mx_a[...] = neg
    diag_tile(0, buf_a)

    def pair(i2, carry):
        fused_block(2 * i2, buf_a, buf_b)
        fused_block(2 * i2 + 1, buf_b, buf_a)
        return carry

    n_fused = n_q - 1
    lax.fori_loop(0, n_fused // 2, pair, 0)
    if n_fused % 2:
        fused_block(n_fused - 1, buf_a, buf_b)
    last = buf_b if n_fused % 2 else buf_a
    acc_sc[...] = jnp.zeros(acc_sc.shape, F32)

    def tail(j2, c):
        prob_tile(last, 2 * j2)
        prob_tile(last, 2 * j2 + 1)
        return c

    lax.fori_loop(0, n_q // 2, tail, 0)
    if n_q % 2:
        prob_tile(last, n_q - 1)
    finish(n_q - 1)


def _attn_prompt(q, kb, vb1, lams, sg, lam_init, blk):
    b, s, _ = q.shape
    small = pl.BlockSpec((1, DA_HEAD_DIM), lambda bi, h: (0, 0))
    per_head = pl.BlockSpec((1, s, LANES), lambda bi, h: (bi, 0, h))
    return pl.pallas_call(
        functools.partial(_attn_prompt_body, lam_init, blk),
        grid=(b, DA_HEADS),
        in_specs=[small, small, small, small,
                  pl.BlockSpec((1, DA_V_DIM), lambda bi, h: (0, 0)),
                  per_head, per_head,
                  pl.BlockSpec((1, s, 2 * DA_V_DIM), lambda bi, h: (bi, 0, h))],
        out_specs=per_head,
        out_shape=jax.ShapeDtypeStruct((b, s, DA_WIDTH), BF16),
        scratch_shapes=[pltpu.VMEM((s // blk, 2 * blk, blk), F32),
                        pltpu.VMEM((s // blk, 2 * blk, blk), F32),
                        pltpu.VMEM((2 * blk, LANES), F32),
                        pltpu.VMEM((2 * blk, LANES), F32),
                        pltpu.VMEM((2 * blk, 2 * DA_V_DIM), F32)],
        compiler_params=_cparams(("parallel", "parallel")),
        name="attn_prompt",
    )(*lams, sg, q, kb, vb1)


def _attn_sample_body(lam_init, n_tok, pages_per_step, pt_ref, lq1_ref, lk1_ref, lq2_ref, lk2_ref,
                      sg_ref, q_ref, kn_ref, vn_ref, *rest):
    kp = rest[:pages_per_step]
    vp = rest[pages_per_step:2 * pages_per_step]
    o_ref = rest[2 * pages_per_step]
    m_sc, l_sc, a_sc = rest[2 * pages_per_step + 1:]
    j = pl.program_id(1)
    nrow = DA_HEADS * 2 * n_tok
    ncol = PAGE_SIZE * DA_HEADS

    @pl.when(j == 0)
    def _():
        m_sc[...] = jnp.full(m_sc.shape, NEG_BIG, F32)
        l_sc[...] = jnp.zeros(l_sc.shape, F32)
        a_sc[...] = jnp.zeros(a_sc.shape, F32)

    q = q_ref[0]
    row = lax.broadcasted_iota(jnp.int32, (nrow, ncol), 0)
    colk = lax.broadcasted_iota(jnp.int32, (nrow, ncol), 1)
    same_head = (colk % DA_HEADS) == (row // (2 * n_tok))

    def update(s_list, v_list):
        m = m_sc[...]
        mn = m
        for s in s_list:
            mn = jnp.maximum(mn, jnp.max(s, axis=-1, keepdims=True))
        alpha = jnp.exp2(m - mn)
        l = alpha * l_sc[...]
        a = alpha * a_sc[...]
        for s, v in zip(s_list, v_list):
            p = jnp.exp2(s - mn)
            l = l + jnp.sum(p, axis=-1, keepdims=True)
            a = a + jnp.dot(p.astype(BF16), v, preferred_element_type=F32)
        m_sc[...] = mn
        l_sc[...] = l
        a_sc[...] = a

    s_list, v_list = [], []
    for i in range(pages_per_step):
        kb = kp[i][0, 0].astype(BF16)
        s_list.append(jnp.where(same_head, _nt_dot(q, kb), NEG_BIG))
        v_list.append(vp[i][0, 0].astype(BF16))
    update(s_list, v_list)

    @pl.when(j == pl.num_programs(1) - 1)
    def _():
        rown = lax.broadcasted_iota(jnp.int32, (nrow, kn_ref.shape[1]), 0)
        coln = lax.broadcasted_iota(jnp.int32, (nrow, kn_ref.shape[1]), 1)
        ok = ((coln < n_tok * DA_HEADS) & ((coln % DA_HEADS) == (rown // (2 * n_tok)))
              & ((coln // DA_HEADS) <= (rown % n_tok)))
        s = jnp.where(ok, _nt_dot(q, kn_ref[0]), NEG_BIG)
        update([s], [vn_ref[0]])
        a = a_sc[...] / l_sc[...]
        lam = _diff_lambda(lq1_ref[...], lk1_ref[...], lq2_ref[...], lk2_ref[...], lam_init)
        for h in range(DA_HEADS):
            base = h * 2 * n_tok
            o = a[base:base + n_tok] - lam * a[base + n_tok:base + 2 * n_tok]
            o = _rmsnorm_rows(o, sg_ref[...], SUBLN_EPS) * (1.0 - lam_init)
            o_ref[0, :, h * LANES:(h + 1) * LANES] = o


def _attn_sample(layer, q_rows, kn, vn, cache_k, cache_v, page_table, lams, sg, lam_init,
                 n_tok, pages_per_step):
    db, nrow, _ = q_rows.shape
    n_pages = page_table.shape[1]
    assert n_pages % pages_per_step == 0
    pt = page_table.reshape(-1)
    small = pl.BlockSpec((1, DA_HEAD_DIM), lambda b, j, p: (0, 0))
    per_seq = lambda b, j, p: (b, 0, 0)

    def page_spec(i):
        return pl.BlockSpec(
            (1, 1, PAGE_SIZE * DA_HEADS, LANES),
            lambda b, j, p: (layer, p[b * n_pages + j * pages_per_step + i], 0, 0))

    pages = [page_spec(i) for i in range(pages_per_step)]
    grid_spec = pltpu.PrefetchScalarGridSpec(
        num_scalar_prefetch=1,
        grid=(db, n_pages // pages_per_step),
        in_specs=[small, small, small, small,
                  pl.BlockSpec((1, DA_V_DIM), lambda b, j, p: (0, 0)),
                  pl.BlockSpec((1, nrow, LANES), per_seq),
                  pl.BlockSpec((1,) + kn.shape[1:], per_seq),
                  pl.BlockSpec((1,) + vn.shape[1:], per_seq)] + pages + pages,
        out_specs=pl.BlockSpec((1, n_tok, DA_WIDTH), per_seq),
        scratch_shapes=[pltpu.VMEM((nrow, 1), F32), pltpu.VMEM((nrow, 1), F32),
                        pltpu.VMEM((nrow, DA_V_DIM), F32)],
    )
    return pl.pallas_call(
        functools.partial(_attn_sample_body, lam_init, n_tok, pages_per_step),
        grid_spec=grid_spec,
        out_shape=jax.ShapeDtypeStruct((db, n_tok, DA_WIDTH), F32),
        compiler_params=_cparams(("parallel", "arbitrary")),
        name="attn_sample",
    )(pt, *lams, sg, q_rows, kn, vn, *([cache_k] * pages_per_step), *([cache_v] * pages_per_step))


def _mix_out_prompt_body(n_chunk, a_ref, u_ref, gv_ref, h_ref, ws_ref, bst_ref, wo_ref, gf_ref,
                         ho_ref, xn_ref, g_sc):
    r = lax.broadcasted_iota(jnp.int32, (CHUNK, CHUNK), 0)
    c = lax.broadcasted_iota(jnp.int32, (CHUNK, CHUNK), 1)
    for g in range(GM_GROUPS):
        w = jnp.where(c <= r, ws_ref[g], 0.0).astype(BF16)
        bias = bst_ref[:, g:g + 1]
        lanes = slice(g * GM_CH, (g + 1) * GM_CH)
        for ci in range(n_chunk):
            rows = slice(ci * CHUNK, (ci + 1) * CHUNK)
            m = jnp.dot(w, gv_ref[rows, lanes], preferred_element_type=F32) + bias
            g_sc[rows, lanes] = (u_ref[rows, lanes].astype(F32) * m).astype(BF16)
    out = (jnp.dot(a_ref[...], wo_ref[:DA_WIDTH, :], preferred_element_type=F32)
           + jnp.dot(g_sc[...], wo_ref[DA_WIDTH:, :], preferred_element_type=F32))
    hn = h_ref[...] + out
    ho_ref[...] = hn
    _put_rows(xn_ref, _rmsnorm_rows(hn, gf_ref[...], RMS_EPS).astype(xn_ref.dtype))


def _mix_out_sample_body(n_tok, n_seq, a_ref, u_ref, gv_ref, h_ref, ws_ref, bs_ref, wo_ref, gf_ref,
                         ho_ref, xn_ref, g_sc):
    for g in range(GM_GROUPS):
        lanes = slice(g * GM_CH, (g + 1) * GM_CH)
        for t in range(n_tok):
            m = jnp.zeros((n_seq, GM_CH), F32) + bs_ref[g:g + 1, t:t + 1]
            for s in range(t + 1):
                m = m + ws_ref[g, t:t + 1, s:s + 1] * gv_ref[s * n_seq:(s + 1) * n_seq, lanes]
            rows = slice(t * n_seq, (t + 1) * n_seq)
            g_sc[rows, lanes] = u_ref[rows, lanes] * m
    out = (jnp.dot(a_ref[...], wo_ref[:DA_WIDTH, :], preferred_element_type=F32)
           + jnp.dot(g_sc[...].astype(BF16), wo_ref[DA_WIDTH:, :], preferred_element_type=F32))
    hn = h_ref[...] + out
    ho_ref[...] = hn
    _put_rows(xn_ref, _rmsnorm_rows(hn, gf_ref[...], RMS_EPS).astype(xn_ref.dtype))


def _mix_out(a, u, gv, h, ws, bs, wo_bf, gf, tm, sample_shape=None, tile_rows=False):
    m, d = h.shape
    row = lambda i: (i, 0)
    tpr = _tiles_per_row(d) if tile_rows else 1
    fixed2 = lambda i: (0, 0)
    fixed3 = lambda i: (0, 0, 0)
    if sample_shape is None:
        body = functools.partial(_mix_out_prompt_body, tm // CHUNK)
        bias = bs.T
        g_dtype = BF16
    else:
        body = functools.partial(_mix_out_sample_body, *sample_shape)
        bias = bs
        g_dtype = F32
    return pl.pallas_call(
        body,
        grid=(m // tm,),
        in_specs=[pl.BlockSpec((tm, DA_WIDTH), row),
                  pl.BlockSpec((tm, GM_WIDTH), row),
                  pl.BlockSpec((tm, GM_WIDTH), row),
                  pl.BlockSpec((tm, d), row),
                  pl.BlockSpec(ws.shape, fixed3),
                  pl.BlockSpec(bias.shape, fixed2),
                  pl.BlockSpec(wo_bf.shape, fixed2),
                  pl.BlockSpec((1, d), fixed2)],
        out_specs=[pl.BlockSpec((tm, d), row), pl.BlockSpec((tm * tpr, d // tpr), row)],
        out_shape=[jax.ShapeDtypeStruct((m, d), F32),
                   jax.ShapeDtypeStruct((m * tpr, d // tpr), F32 if tile_rows else BF16)],
        scratch_shapes=[pltpu.VMEM((tm, GM_WIDTH), g_dtype)],
        compiler_params=_cparams(("parallel",)),
        name="mix_out",
    )(a, u, gv, h, ws, bias, wo_bf, gf)


def _ffn_body(xn_ref, h_ref, wg_ref, wu_ref, wd_ref, o_ref):
    x = xn_ref[...]
    g = jnp.dot(x, wg_ref[...], preferred_element_type=F32)
    u = jnp.dot(x, wu_ref[...], preferred_element_type=F32)
    hid = (g * jax.nn.sigmoid(g) * u).astype(BF16)
    o_ref[...] = h_ref[...] + jnp.dot(hid, wd_ref[...], preferred_element_type=F32)


def _ffn(xn, h, wg, wu, wd, tm):
    m, d = h.shape
    ff = wg.shape[1]
    resident = dict(pipeline_mode=pl.Buffered(1))
    return pl.pallas_call(
        _ffn_body,
        grid=(m // tm,),
        in_specs=[pl.BlockSpec((tm, d), lambda i: (i, 0)),
                  pl.BlockSpec((tm, d), lambda i: (i, 0)),
                  pl.BlockSpec((d, ff), lambda i: (0, 0), **resident),
                  pl.BlockSpec((d, ff), lambda i: (0, 0), **resident),
                  pl.BlockSpec((ff, d), lambda i: (0, 0), **resident)],
        out_specs=pl.BlockSpec((tm, d), lambda i: (i, 0)),
        out_shape=jax.ShapeDtypeStruct((m, d), F32),
        compiler_params=_cparams(("parallel",)),
        name="ffn",
    )(xn, h, wg, wu, wd)


R_E1, R_E2, R_G1, R_G2, R_RANK1, R_RANK2 = range(6)


def _router_body(n_exp, xn_ref, wr_ref, base_ref, o_ref, cnt_ref, cnt_sc):
    i = pl.program_id(0)

    @pl.when(i == 0)
    def _():
        cnt_sc[...] = base_ref[...]

    d = wr_ref.shape[0]
    tm = xn_ref.shape[0] // _tiles_per_row(d)
    logits = jnp.dot(_get_rows(xn_ref, tm, d), wr_ref[...], preferred_element_type=F32,
                     precision=lax.Precision.HIGHEST)
    col = lax.broadcasted_iota(jnp.int32, logits.shape, 1)
    logits = jnp.where(col < n_exp, logits, -jnp.inf)
    big = jnp.int32(LANES)
    v1 = jnp.max(logits, axis=-1, keepdims=True)
    i1 = jnp.min(jnp.where(logits == v1, col, big), axis=-1, keepdims=True)
    rest = jnp.where(col == i1, -jnp.inf, logits)
    v2 = jnp.max(rest, axis=-1, keepdims=True)
    i2 = jnp.min(jnp.where(rest == v2, col, big), axis=-1, keepdims=True)
    e = jnp.exp(v2 - v1)
    g1 = 1.0 / (1.0 + e)
    g2 = e / (1.0 + e)
    hit = (col == i1) | (col == i2)
    r = lax.broadcasted_iota(jnp.int32, (tm, tm), 0)
    c = lax.broadcasted_iota(jnp.int32, (tm, tm), 1)
    below = jnp.where(c < r, 1.0, 0.0).astype(BF16)
    prefix = jnp.dot(below, jnp.where(hit, 1.0, 0.0).astype(BF16), preferred_element_type=F32)
    rank = prefix + cnt_sc[...]
    r1 = jnp.sum(jnp.where(col == i1, rank, 0.0), axis=-1, keepdims=True)
    r2 = jnp.sum(jnp.where(col == i2, rank, 0.0), axis=-1, keepdims=True)
    cnt_sc[...] += jnp.sum(jnp.where(hit, 1.0, 0.0), axis=0, keepdims=True)
    rec = jnp.zeros(logits.shape, F32)
    for idx, val in ((R_E1, i1.astype(F32)), (R_E2, i2.astype(F32)), (R_G1, g1), (R_G2, g2),
                     (R_RANK1, r1), (R_RANK2, r2)):
        rec = jnp.where(col == idx, val, rec)
    o_ref[...] = rec
    cnt_ref[...] = cnt_sc[...]


def _router(xn_t, wr_pad, base_counts, n_exp, tm):
    tpr = _tiles_per_row(wr_pad.shape[0])
    m = xn_t.shape[0] // tpr
    return pl.pallas_call(
        functools.partial(_router_body, n_exp),
        grid=(m // tm,),
        in_specs=[pl.BlockSpec((tm * tpr, LANES), lambda i: (i, 0)),
                  pl.BlockSpec(wr_pad.shape, lambda i: (0, 0)),
                  pl.BlockSpec((1, LANES), lambda i: (0, 0))],
        out_specs=[pl.BlockSpec((tm, LANES), lambda i: (i, 0)),
                   pl.BlockSpec((1, LANES), lambda i: (0, 0))],
        out_shape=[jax.ShapeDtypeStruct((m, LANES), F32), jax.ShapeDtypeStruct((1, LANES), F32)],
        scratch_shapes=[pltpu.VMEM((1, LANES), F32)],
        compiler_params=_cparams(("arbitrary",)),
        name="router",
    )(xn_t, wr_pad, base_counts)


DMA_UNROLL = 8


def _tile_copy(src, src_row, dst, dst_row, tpr, sem):
    return pltpu.make_async_copy(src.at[pl.ds(pl.multiple_of(src_row * tpr, tpr), tpr), :],
                                 dst.at[pl.ds(pl.multiple_of(dst_row * tpr, tpr), tpr), :], sem)


def _dispatch_body(tm, tpr, slot_ref, xn_ref, xs_in_ref, xs_ref, sem):
    del xs_in_ref

    def start(t, c):
        for k in range(TOP_K):
            _tile_copy(xn_ref, t, xs_ref, slot_ref[0, 0, TOP_K * t + k], tpr, sem).start(
                priority=k % 2)
        return c

    def wait(t, c):
        for k in range(TOP_K):
            _tile_copy(xn_ref, t, xs_ref, t, tpr, sem).wait()
        return c

    lax.fori_loop(0, tm, start, 0, unroll=DMA_UNROLL)
    lax.fori_loop(0, tm, wait, 0, unroll=DMA_UNROLL)


def _dispatch(slots, xn_t, xs_t, tm, tpr):
    m = xn_t.shape[0] // tpr
    slots3 = slots.reshape(m // tm, 1, tm * TOP_K)
    return pl.pallas_call(
        functools.partial(_dispatch_body, tm, tpr),
        grid=(m // tm,),
        in_specs=[pl.BlockSpec((1, 1, tm * TOP_K), lambda i: (i, 0, 0), memory_space=pltpu.SMEM),
                  pl.BlockSpec((tm * tpr, LANES), lambda i: (i, 0)),
                  pl.BlockSpec(memory_space=pl.ANY)],
        out_specs=pl.BlockSpec(memory_space=pl.ANY),
        out_shape=jax.ShapeDtypeStruct(xs_t.shape, xs_t.dtype),
        scratch_shapes=[pltpu.SemaphoreType.DMA(())],
        input_output_aliases={2: 0},
        compiler_params=_cparams(("arbitrary",)),
        name="moe_dispatch",
    )(slots3, xn_t, xs_t)


def _combine_body(tm, tpr, slot_cur_ref, slot_nxt_ref, rec_ref, h_ref, g_ref, ys_ref, o_ref,
                  buf, sem):
    i = pl.program_id(0)
    n = pl.num_programs(0)
    d = h_ref.shape[1]

    def fetch(slot_ref, half):
        def start(t, c):
            for k in range(TOP_K):
                _tile_copy(ys_ref, slot_ref[0, 0, TOP_K * t + k], buf.at[half, k], t, tpr,
                           sem.at[half]).start(priority=k % 2)
            return c
        lax.fori_loop(0, tm, start, 0, unroll=DMA_UNROLL)

    def wait_block(half):
        def wait(t, c):
            for k in range(TOP_K):
                _tile_copy(ys_ref, t, buf.at[half, k], t, tpr, sem.at[half]).wait()
            return c
        lax.fori_loop(0, tm, wait, 0, unroll=DMA_UNROLL)

    def step(half):
        @pl.when(i == 0)
        def _():
            fetch(slot_cur_ref, half)

        @pl.when(i + 1 < n)
        def _():
            fetch(slot_nxt_ref, 1 - half)

        wait_block(half)
        rec = rec_ref[...]
        y = (rec[:, R_G1:R_G1 + 1] * _get_rows(buf.at[half, 0], tm, d)
             + rec[:, R_G2:R_G2 + 1] * _get_rows(buf.at[half, 1], tm, d))
        o_ref[...] = _rmsnorm_rows(h_ref[...] + y, g_ref[...], RMS_EPS)

    @pl.when(i % 2 == 0)
    def _():
        step(0)

    @pl.when(i % 2 == 1)
    def _():
        step(1)


def _combine_norm(slots, rec, h, g, ys_t, tm):
    m, d = h.shape
    tpr = _tiles_per_row(d)
    n = m // tm
    slots3 = slots.reshape(n, 1, tm * TOP_K)
    row = lambda i: (i, 0)
    return pl.pallas_call(
        functools.partial(_combine_body, tm, tpr),
        grid=(n,),
        in_specs=[pl.BlockSpec((1, 1, tm * TOP_K), lambda i: (i, 0, 0), memory_space=pltpu.SMEM),
                  pl.BlockSpec((1, 1, tm * TOP_K), lambda i: (jnp.minimum(i + 1, n - 1), 0, 0),
                               memory_space=pltpu.SMEM),
                  pl.BlockSpec((tm, LANES), row),
                  pl.BlockSpec((tm, d), row),
                  pl.BlockSpec((1, d), lambda i: (0, 0)),
                  pl.BlockSpec(memory_space=pl.ANY)],
        out_specs=pl.BlockSpec((tm, d), row),
        out_shape=jax.ShapeDtypeStruct((m, d), F32),
        scratch_shapes=[pltpu.VMEM((2, TOP_K, tm * tpr, LANES), F32),
                        pltpu.SemaphoreType.DMA((2,))],
        compiler_params=_cparams(("arbitrary",)),
        name="moe_combine_norm",
    )(slots3, slots3, rec, h, g, ys_t)


def _moe_body(tm, be_ref, nv_ref, xs_ref, wg_ref, wu_ref, wd_ref, o_ref, x_sc, acc_sc):
    i = pl.program_id(0)
    f = pl.program_id(1)
    last = pl.num_programs(1) - 1
    live = i < nv_ref[0]
    d = x_sc.shape[1]

    @pl.when(live & (f == 0))
    def _():
        acc_sc[...] = jnp.zeros(acc_sc.shape, F32)
        x_sc[...] = _get_rows(xs_ref, tm, d).astype(BF16)

    @pl.when(live)
    def _():
        x = x_sc[...]
        g = jnp.dot(x, wg_ref[0], preferred_element_type=F32)
        u = jnp.dot(x, wu_ref[0], preferred_element_type=F32)
        hid = (g * jax.nn.sigmoid(g) * u).astype(BF16)
        acc_sc[...] += jnp.dot(hid, wd_ref[0], preferred_element_type=F32)

    @pl.when(live & (f == last))
    def _():
        _put_rows(o_ref, acc_sc[...])

    @pl.when(jnp.logical_not(live) & (f == last))
    def _():
        o_ref[...] = jnp.zeros(o_ref.shape, F32)


def _moe_experts(xs_t, block_expert, n_valid, wg, wu, wd, tm, tf):
    d, ff = wg.shape[1], wg.shape[2]
    tpr = _tiles_per_row(d)
    p = xs_t.shape[0] // tpr
    rows = pl.BlockSpec((tm * tpr, LANES), lambda i, f, be, nv: (i, 0))
    grid_spec = pltpu.PrefetchScalarGridSpec(
        num_scalar_prefetch=2,
        grid=(p // tm, ff // tf),
        in_specs=[rows,
                  pl.BlockSpec((1, d, tf), lambda i, f, be, nv: (be[i], 0, f)),
                  pl.BlockSpec((1, d, tf), lambda i, f, be, nv: (be[i], 0, f)),
                  pl.BlockSpec((1, tf, d), lambda i, f, be, nv: (be[i], f, 0))],
        out_specs=rows,
        scratch_shapes=[pltpu.VMEM((tm, d), BF16), pltpu.VMEM((tm, d), F32)],
    )
    return pl.pallas_call(
        functools.partial(_moe_body, tm),
        grid_spec=grid_spec,
        out_shape=jax.ShapeDtypeStruct(xs_t.shape, F32),
        compiler_params=_cparams(("parallel", "arbitrary")),
        name="moe_experts",
    )(block_expert, n_valid, xs_t, wg, wu, wd)


def _moe_final(groups, w_router, wg, wu, wd, final_g, tm, tf):
    d = w_router.shape[0]
    n_exp = w_router.shape[1]
    n_tok = sum(h.shape[0] for _, h, _ in groups)
    wr_pad = jnp.zeros((d, LANES), F32).at[:, :n_exp].set(w_router)
    counts = jnp.zeros((1, LANES), F32)
    recs = []
    for xn, h, _ in groups:
        rec, counts = _router(xn, wr_pad, counts, n_exp, _pick(h.shape[0], 512))
        recs.append(rec)
    cnt = counts[0, :n_exp].astype(jnp.int32)
    padded = ((cnt + tm - 1) // tm) * tm
    group_end = jnp.cumsum(padded)
    group_start = group_end - padded
    n_slots = ((n_tok * TOP_K + n_exp * (tm - 1)) // tm) * tm
    n_blocks = n_slots // tm
    block_start = jnp.arange(n_blocks, dtype=jnp.int32) * tm
    n_valid = (group_end[-1] // tm).astype(jnp.int32)
    block_expert = jnp.sum((block_start[:, None] >= group_end[None, :]).astype(jnp.int32), axis=1)
    last_valid = jnp.sum((jnp.maximum(n_valid - 1, 0) * tm >= group_end).astype(jnp.int32))
    block_expert = jnp.minimum(jnp.where(block_start < n_valid * tm, block_expert, last_valid),
                               n_exp - 1).astype(jnp.int32)

    def slots_of(rec):
        e = rec[:, R_E1:R_E2 + 1].astype(jnp.int32)
        rank = rec[:, R_RANK1:R_RANK2 + 1].astype(jnp.int32)
        start = jnp.sum(jnp.where(e[..., None] == jnp.arange(n_exp), group_start, 0), axis=-1)
        return (start + rank).astype(jnp.int32)

    slots = [slots_of(rec) for rec in recs]
    tpr = _tiles_per_row(d)
    xs = jnp.zeros((n_slots * tpr, LANES), F32)
    for (xn, _, tb), sl in zip(groups, slots):
        xs = _dispatch(sl, xn, xs, _pick(sl.shape[0], 4 * tb), tpr)
    ys = _moe_experts(xs, block_expert, n_valid.reshape(1), wg, wu, wd, tm, tf)
    return [_combine_norm(sl, rec, h, final_g, ys, tb)
            for (_, h, tb), sl, rec in zip(groups, slots, recs)]


def _pick(total, pref):
    t = min(total, pref)
    while total % t:
        t //= 2
    return t


def kernel(x_prompt, x_sample, cache_k, cache_v, page_table, w_in, w_out, norm_mix_g, norm_ffn_g,
           lam_q1, lam_k1, lam_q2, lam_k2, subln_g, gm_ln_g, gm_ln_b, gm_ws, gm_bs,
           ffn_wg, ffn_wu, ffn_wd, moe_router, moe_wg, moe_wu, moe_wd, final_norm_g):
    b, s, d = x_prompt.shape
    db, t = x_sample.shape[:2]
    depth = w_in.shape[0]
    n_pages = page_table.shape[1]
    past = n_pages * PAGE_SIZE
    n_p, n_s = b * s, db * t
    assert depth == 2 and s % CHUNK == 0

    tm_p = _pick(s, 512)
    blk = _pick(s, 512)
    pages_per_step = _pick(n_pages, 32)

    hp = x_prompt.reshape(n_p, d)
    hs = x_sample.transpose(1, 0, 2).reshape(n_s, d)
    tab_p = _rope_tables(jnp.arange(s, dtype=F32))
    pos_s = jnp.arange(t, dtype=F32) + jnp.float32(past)
    tab_s = _rope_tables(jnp.repeat(pos_s, db))
    ck = cache_k.reshape(cache_k.shape[0], cache_k.shape[1], PAGE_SIZE * DA_HEADS, LANES)
    cv = cache_v.reshape(cache_v.shape[0], cache_v.shape[1], PAGE_SIZE * DA_HEADS, LANES)
    lane = jnp.arange(LANES)
    comp_mask = jnp.stack([lane < DA_HEAD_DIM, lane >= DA_HEAD_DIM])
    new_rows = PAGE_SIZE
    assert t * DA_HEADS <= new_rows

    outs = {k: [] for k in ("ks", "vs", "gv")}
    y_p = y_s = k_all = v_all = None
    for l in range(depth):
        lam_init = 0.8 - 0.6 * math.exp(-0.3 * l)
        w_in_bf = w_in[l].astype(BF16)
        w_out_bf = w_out[l].astype(BF16)
        g_mix = norm_mix_g[l][None]
        g_ffn = norm_ffn_g[l][None]
        lng, lnb = gm_ln_g[l][None], gm_ln_b[l][None]
        lams = (lam_q1[l][None], lam_k1[l][None], lam_q2[l][None], lam_k2[l][None])
        sg = subln_g[l][None]

        q, k_all, v_all, kb, vb, u, gv = _proj(hp, g_mix, w_in_bf, tab_p, lng, lnb, tm_p, BF16, BF16,
                                               kv_stack=(l, depth, k_all, v_all))
        a = _attn_prompt(q.reshape(b, s, QK_COLS), kb.reshape(b, s, QK_COLS),
                         vb.reshape(b, s, 2 * DA_WIDTH), lams, sg, lam_init, blk)
        moe_layer = l % 2 == 1
        hp, xn_p = _mix_out(a.reshape(n_p, DA_WIDTH), u, gv, hp, gm_ws[l], gm_bs[l], w_out_bf, g_ffn,
                            tm_p, tile_rows=moe_layer)

        sq, sk, sv, skb, svb, su, sgv = _proj(hs, g_mix, w_in_bf, tab_s, lng, lnb, n_s, F32, F32)
        q5 = sq.reshape(t, db, DA_HEADS, 1, LANES).transpose(1, 2, 3, 0, 4)
        q_rows = jnp.where(comp_mask[None, None, :, None, :], q5, jnp.zeros((), BF16))
        q_rows = q_rows.reshape(db, DA_HEADS * 2 * t, LANES)
        pad = ((0, 0), (0, new_rows - t * DA_HEADS), (0, 0))
        kn = jnp.pad(skb.reshape(t, db, DA_HEADS, LANES).transpose(1, 0, 2, 3).reshape(db, t * DA_HEADS, LANES), pad)
        svb = svb.reshape(t, db, DA_HEADS, 2 * DA_V_DIM)[..., :DA_V_DIM]
        vn = jnp.pad(svb.transpose(1, 0, 2, 3).reshape(db, t * DA_HEADS, LANES), pad)
        sa = _attn_sample(l, q_rows, kn, vn, ck, cv, page_table, lams, sg, lam_init, t, pages_per_step)
        sa = sa.transpose(1, 0, 2).reshape(n_s, DA_WIDTH).astype(BF16)
        hs, xn_s = _mix_out(sa, su, sgv, hs, gm_ws[l], gm_bs[l], w_out_bf, g_ffn, n_s,
                            sample_shape=(t, db), tile_rows=moe_layer)

        outs["ks"].append(sk.reshape(t, db, DA_HEADS, 2 * DA_HEAD_DIM).transpose(1, 0, 2, 3))
        outs["vs"].append(sv.reshape(t, db, DA_HEADS, DA_V_DIM).transpose(1, 0, 2, 3))
        outs["gv"].append(sgv.reshape(t, db, GM_GROUPS, GM_CH).transpose(1, 0, 2, 3))

        j = l // 2
        if l % 2 == 0:
            wg, wu, wd = ffn_wg[j].astype(BF16), ffn_wu[j].astype(BF16), ffn_wd[j].astype(BF16)
            hp = _ffn(xn_p, hp, wg, wu, wd, tm_p)
            hs = _ffn(xn_s, hs, wg, wu, wd, n_s)
        else:
            wg, wu, wd = moe_wg[j].astype(BF16), moe_wu[j].astype(BF16), moe_wd[j].astype(BF16)
            y_p, y_s = _moe_final([(xn_p, hp, _pick(n_p, 256)), (xn_s, hs, n_s)], moe_router[j],
                                  wg, wu, wd, final_norm_g[None], 512, _pick(wg.shape[2], 1792))

    y_prompt = y_p.reshape(b, s, d)
    y_sample = y_s.reshape(t, db, d).transpose(1, 0, 2)
    return (y_prompt, y_sample,
            k_all.reshape(depth, b, s, DA_HEADS, 2 * DA_HEAD_DIM),
            v_all.reshape(depth, b, s, DA_HEADS, DA_V_DIM),
            jnp.stack(outs["ks"]), jnp.stack(outs["vs"]), jnp.stack(outs["gv"]))
```

```python
import functools
import math

import jax
import jax.numpy as jnp
from jax import lax
from jax.experimental import pallas as pl
from jax.experimental.pallas import tpu as pltpu

F32 = jnp.float32
BF16 = jnp.bfloat16

DA_HEADS = 4
DA_HEAD_DIM = 64
DA_V_DIM = 2 * DA_HEAD_DIM
QK_COLS = DA_HEADS * 2 * DA_HEAD_DIM
DA_WIDTH = DA_HEADS * DA_V_DIM
ROT_DIM = DA_HEAD_DIM // 4
ROPE_THETA = 500000.0
GM_GROUPS = 4
GM_CH = 128
GM_WIDTH = GM_GROUPS * GM_CH
CHUNK = 128
PAGE_SIZE = 128
TOP_K = 2
RMS_EPS = 1e-6
SUBLN_EPS = 1e-5
LN_EPS = 1e-5
NEG_BIG = -1e30
LOG2_E = math.log2(math.e)
LANES = 128
V7X_MXU_COLS = 256
V7X_VMEM_BYTES = 64 * 1024 * 1024
VMEM_LIMIT = V7X_VMEM_BYTES * 13 // 16
DMA_UNROLL = 8


def _pick(total, pref):
    t = min(total, pref)
    while total % t:
        t //= 2
    return t


def _tile_plan(s, n_p, n_pages, d_ff_expert):
    rows = _pick(s, 512)
    assert rows % CHUNK == 0
    expert_cols = _pick(d_ff_expert, 1792)
    assert expert_cols % V7X_MXU_COLS == 0, "partial MXU column tiles waste matmul passes"
    return dict(
        rows=rows,
        attn_blk=_pick(s, 512),
        pages_per_step=_pick(n_pages, 32),
        expert_rows=512,
        expert_cols=expert_cols,
        combine_rows=_pick(n_p, 256),
        cast_rows=512,
    )


def _cparams(sem):
    return pltpu.CompilerParams(dimension_semantics=sem, vmem_limit_bytes=VMEM_LIMIT)


def _cast_body(x_ref, o_ref):
    o_ref[...] = x_ref[...].astype(o_ref.dtype)


def _to_bf16(w, rows):
    x = w.reshape(-1, w.shape[-1])
    tr = _pick(x.shape[0], rows)
    out = pl.pallas_call(
        _cast_body,
        grid=(x.shape[0] // tr,),
        in_specs=[pl.BlockSpec((tr, x.shape[1]), lambda i: (i, 0))],
        out_specs=pl.BlockSpec((tr, x.shape[1]), lambda i: (i, 0)),
        out_shape=jax.ShapeDtypeStruct(x.shape, BF16),
        compiler_params=_cparams(("parallel",)),
        name="to_bf16",
    )(x)
    return out.reshape(w.shape)


def _nt_dot(a, b):
    return lax.dot_general(a, b, (((1,), (1,)), ((), ())), preferred_element_type=F32)


def _rmsnorm_rows(x, g, eps):
    return x * lax.rsqrt(jnp.mean(x * x, axis=-1, keepdims=True) + eps) * g


def _tiles_per_row(d):
    return d // LANES


def _put_rows(ref, x):
    n, d = x.shape
    if ref.shape == x.shape:
        ref[...] = x
        return
    tpr = _tiles_per_row(d)
    assert ref.shape == (n * tpr, LANES)
    for c in range(tpr):
        ref[pl.ds(c, n, stride=tpr), :] = x[:, c * LANES:(c + 1) * LANES]


def _get_rows(ref, n, d):
    tpr = _tiles_per_row(d)
    return jnp.concatenate([ref[pl.ds(c, n, stride=tpr), :] for c in range(tpr)], axis=1)


def _gelu_exact(x):
    return 0.5 * x * (1.0 + lax.erf(x * (2.0 ** -0.5)))


def _diff_lambda(lq1, lk1, lq2, lk2, lam_init):
    a = jnp.exp(jnp.sum(lq1 * lk1, axis=-1, keepdims=True))
    b = jnp.exp(jnp.sum(lq2 * lk2, axis=-1, keepdims=True))
    return a - b + lam_init


def _put_head(ref, t, x):
    if ref.shape[-1] == DA_HEADS * LANES:
        ref[:, t * LANES:(t + 1) * LANES] = x
    else:
        for j in range(ref.shape[0]):
            ref.at[j][pl.ds(t, x.shape[0], stride=DA_HEADS), :] = x


def _proj_body(n_alias, h_ref, g_ref, w_ref, cos_ref, sa_ref, sb_ref, lng_ref, lnb_ref, *rest):
    q_ref, k_ref, v_ref, kb_ref, vb_ref, u_ref, gv_ref = rest[n_alias:]
    h = h_ref[...]
    xn = _rmsnorm_rows(h, g_ref[...], RMS_EPS).astype(BF16)
    z = jnp.dot(xn, w_ref[...], preferred_element_type=F32)
    cos, sa, sb = cos_ref[...], sa_ref[...], sb_ref[...]

    def cols(lo, width):
        return z[:, lo:lo + width]

    def rope(x):
        return (x * cos + pltpu.roll(x, ROT_DIM // 2, 1) * sa
                + pltpu.roll(x, LANES - ROT_DIM // 2, 1) * sb)

    scale = DA_HEAD_DIM ** -0.5 * LOG2_E
    ones = jnp.ones((h.shape[0], DA_V_DIM), vb_ref.dtype)
    zq = cols(0, QK_COLS)
    for t in range(DA_HEADS):
        sl = slice(t * LANES, (t + 1) * LANES)
        q_ref[:, sl] = (rope(zq[:, sl]) * scale).astype(q_ref.dtype)
    zk = cols(QK_COLS, QK_COLS)
    for t in range(DA_HEADS):
        sl = slice(t * LANES, (t + 1) * LANES)
        kt = rope(zk[:, sl])
        _put_head(k_ref, t, kt)
        kb_ref[:, sl] = kt.astype(kb_ref.dtype)
    zv = cols(2 * QK_COLS, DA_WIDTH)
    for t in range(DA_HEADS):
        vt = zv[:, t * DA_V_DIM:(t + 1) * DA_V_DIM]
        _put_head(v_ref, t, vt)
        vb_ref[:, 2 * t * DA_V_DIM:(2 * t + 1) * DA_V_DIM] = vt.astype(vb_ref.dtype)
        vb_ref[:, (2 * t + 1) * DA_V_DIM:(2 * t + 2) * DA_V_DIM] = ones
    u_ref[...] = _gelu_exact(cols(2 * QK_COLS + DA_WIDTH, GM_WIDTH)).astype(u_ref.dtype)
    gvr = _gelu_exact(cols(2 * QK_COLS + DA_WIDTH + GM_WIDTH, GM_WIDTH))
    mu = jnp.mean(gvr, axis=-1, keepdims=True)
    d = gvr - mu
    var = jnp.mean(d * d, axis=-1, keepdims=True)
    gv = d * lax.rsqrt(var + LN_EPS) * lng_ref[...] + lnb_ref[...]
    gv_ref[...] = gv.astype(gv_ref.dtype)


def _proj(h, g, w_bf, tables, lng, lnb, tm, u_dtype, gv_dtype, kv_stack=None):
    m, d = h.shape
    cos_t, sa_t, sb_t = tables
    ntab = cos_t.shape[0] // tm
    row = lambda i: (i, 0)
    fixed = lambda i: (0, 0)
    tab = lambda i: (i % ntab, 0)
    wide = lambda dt: jax.ShapeDtypeStruct((m, QK_COLS), dt)
    in_specs = [
        pl.BlockSpec((tm, d), row),
        pl.BlockSpec((1, d), fixed),
        pl.BlockSpec(w_bf.shape, fixed),
        pl.BlockSpec((tm, LANES), tab),
        pl.BlockSpec((tm, LANES), tab),
        pl.BlockSpec((tm, LANES), tab),
        pl.BlockSpec((1, GM_WIDTH), fixed),
        pl.BlockSpec((1, GM_WIDTH), fixed),
    ]
    args = [h, g, w_bf, cos_t, sa_t, sb_t, lng, lnb]
    kv_spec = pl.BlockSpec((tm, QK_COLS), row)
    kv_shape = wide(F32)
    aliases = {}
    if kv_stack is not None:
        layer, depth, k_all, v_all = kv_stack
        kv_shape = jax.ShapeDtypeStruct((depth, m * DA_HEADS, LANES), F32)
        if k_all is None:
            assert layer == 0
            kv_spec = pl.BlockSpec((depth, tm * DA_HEADS, LANES), lambda i: (0, i, 0))
        else:
            kv_spec = pl.BlockSpec((1, tm * DA_HEADS, LANES), lambda i: (layer, i, 0))
            aliases = {len(args): 1, len(args) + 1: 2}
            in_specs += [pl.BlockSpec(memory_space=pl.ANY)] * 2
            args += [k_all, v_all]
    return pl.pallas_call(
        functools.partial(_proj_body, len(aliases)),
        grid=(m // tm,),
        in_specs=in_specs,
        out_specs=[pl.BlockSpec((tm, QK_COLS), row), kv_spec, kv_spec,
                   pl.BlockSpec((tm, QK_COLS), row), pl.BlockSpec((tm, 2 * DA_WIDTH), row),
                   pl.BlockSpec((tm, GM_WIDTH), row), pl.BlockSpec((tm, GM_WIDTH), row)],
        out_shape=[wide(BF16), kv_shape, kv_shape, wide(BF16),
                   jax.ShapeDtypeStruct((m, 2 * DA_WIDTH), BF16), wide(u_dtype), wide(gv_dtype)],
        input_output_aliases=aliases,
        compiler_params=_cparams(("parallel",)),
        name="proj",
    )(*args)


def _rope_tables(pos):
    inv = ROPE_THETA ** (-jnp.arange(0, ROT_DIM, 2, dtype=F32) / ROT_DIM)
    ang = pos[:, None] * inv[None, :]
    c, s = jnp.cos(ang), jnp.sin(ang)
    j = jnp.arange(LANES) % DA_HEAD_DIM
    first = j < ROT_DIM // 2
    second = (j >= ROT_DIM // 2) & (j < ROT_DIM)
    idx = j % (ROT_DIM // 2)
    cg, sg = c[:, idx], s[:, idx]
    cos_t = jnp.where(first | second, cg, 1.0)
    sa_t = jnp.where(second, sg, 0.0)
    sb_t = jnp.where(first, -sg, 0.0)
    return cos_t, sa_t, sb_t


def _attn_prompt_body(lam_init, blk, lq1_ref, lk1_ref, lq2_ref, lk2_ref, sg_ref,
                      q_ref, k_ref, v_ref, o_ref, s_a, s_b, mx_a, mx_b, acc_sc):
    n_q = q_ref.shape[1] // blk
    buf_a, buf_b = (s_a, mx_a), (s_b, mx_b)
    lane = lax.broadcasted_iota(jnp.int32, (blk, LANES), 1)
    r2 = lax.broadcasted_iota(jnp.int32, (2 * blk, blk), 0)
    c2 = lax.broadcasted_iota(jnp.int32, (2 * blk, blk), 1)
    causal = c2 <= jnp.where(r2 >= blk, r2 - blk, r2)
    lam = _diff_lambda(lq1_ref[...], lk1_ref[...], lq2_ref[...], lk2_ref[...], lam_init)
    gain = sg_ref[...] * (1.0 - lam_init)

    def fold(s):
        out = s[:, :LANES]
        for i in range(1, blk // LANES):
            out = jnp.maximum(out, s[:, i * LANES:(i + 1) * LANES])
        return out

    def stacked_q(qi):
        q = q_ref[0, pl.ds(pl.multiple_of(qi * blk, blk), blk), :]
        zero = jnp.zeros_like(q)
        return jnp.concatenate([jnp.where(lane < DA_HEAD_DIM, q, zero),
                                jnp.where(lane >= DA_HEAD_DIM, q, zero)], axis=0)

    def score_tile(q_st, buf, j):
        s_ref, mx_ref = buf
        s = _nt_dot(q_st, k_ref[0, pl.ds(pl.multiple_of(j * blk, blk), blk), :])
        s_ref[j] = s
        mx_ref[...] = jnp.maximum(mx_ref[...], fold(s))

    def diag_tile(qi, buf):
        s_ref, mx_ref = buf
        q0 = pl.multiple_of(qi * blk, blk)
        s = jnp.where(causal, _nt_dot(stacked_q(qi), k_ref[0, pl.ds(q0, blk), :]), NEG_BIG)
        s_ref[qi] = s
        m = jnp.max(jnp.maximum(mx_ref[...], fold(s)), axis=1, keepdims=True)
        mx_ref[...] = jnp.broadcast_to(m, (2 * blk, LANES))

    def prob_tile(buf, j):
        s_ref, mx_ref = buf
        mb = mx_ref[...]
        p = jnp.exp2(s_ref[j] - jnp.concatenate([mb] * (blk // LANES), axis=1)).astype(BF16)
        acc_sc[...] += jnp.dot(p, v_ref[0, pl.ds(pl.multiple_of(j * blk, blk), blk), :],
                               preferred_element_type=F32)

    def finish(qi):
        acc = acc_sc[...]
        o = (acc[:blk, :DA_V_DIM] / acc[:blk, DA_V_DIM:]
             - lam * (acc[blk:, :DA_V_DIM] / acc[blk:, DA_V_DIM:]))
        o = _rmsnorm_rows(o, gain, SUBLN_EPS)
        o_ref[0, pl.ds(pl.multiple_of(qi * blk, blk), blk), :] = o.astype(o_ref.dtype)

    neg = jnp.full((2 * blk, LANES), NEG_BIG, F32)

    def fused_block(qi, cur, nxt):
        q_next = stacked_q(qi + 1)
        nxt[1][...] = neg
        acc_sc[...] = jnp.zeros(acc_sc.shape, F32)

        def tile(j):
            prob_tile(cur, j)
            score_tile(q_next, nxt, j)

        def two_tiles(j2, c):
            tile(2 * j2)
            tile(2 * j2 + 1)
            return c

        n_t = qi + 1
        lax.fori_loop(0, n_t // 2, two_tiles, 0)

        @pl.when(n_t % 2 == 1)
        def _():
            tile(n_t - 1)

        finish(qi)
        diag_tile(qi + 1, nxt)

    mx_a[...] = neg
    diag_tile(0, buf_a)

    def pair(i2, carry):
        fused_block(2 * i2, buf_a, buf_b)
        fused_block(2 * i2 + 1, buf_b, buf_a)
        return carry

    n_fused = n_q - 1
    lax.fori_loop(0, n_fused // 2, pair, 0)
    if n_fused % 2:
        fused_block(n_fused - 1, buf_a, buf_b)
    last = buf_b if n_fused % 2 else buf_a
    acc_sc[...] = jnp.zeros(acc_sc.shape, F32)

    def tail(j2, c):
        prob_tile(last, 2 * j2)
        prob_tile(last, 2 * j2 + 1)
        return c

    lax.fori_loop(0, n_q // 2, tail, 0)
    if n_q % 2:
        prob_tile(last, n_q - 1)
    finish(n_q - 1)


def _attn_prompt(q, kb, vb1, lams, sg, lam_init, blk):
    b, s, _ = q.shape
    small = pl.BlockSpec((1, DA_HEAD_DIM), lambda bi, h: (0, 0))
    per_head = pl.BlockSpec((1, s, LANES), lambda bi, h: (bi, 0, h))
    return pl.pallas_call(
        functools.partial(_attn_prompt_body, lam_init, blk),
        grid=(b, DA_HEADS),
        in_specs=[small, small, small, small,
                  pl.BlockSpec((1, DA_V_DIM), lambda bi, h: (0, 0)),
                  per_head, per_head,
                  pl.BlockSpec((1, s, 2 * DA_V_DIM), lambda bi, h: (bi, 0, h))],
        out_specs=per_head,
        out_shape=jax.ShapeDtypeStruct((b, s, DA_WIDTH), BF16),
        scratch_shapes=[pltpu.VMEM((s // blk, 2 * blk, blk), F32),
                        pltpu.VMEM((s // blk, 2 * blk, blk), F32),
                        pltpu.VMEM((2 * blk, LANES), F32),
                        pltpu.VMEM((2 * blk, LANES), F32),
                        pltpu.VMEM((2 * blk, 2 * DA_V_DIM), F32)],
        compiler_params=_cparams(("parallel", "parallel")),
        name="attn_prompt",
    )(*lams, sg, q, kb, vb1)


def _attn_sample_body(lam_init, n_tok, pages_per_step, pt_ref, lq1_ref, lk1_ref, lq2_ref, lk2_ref,
                      sg_ref, q_ref, kn_ref, vn_ref, *rest):
    kp = rest[:pages_per_step]
    vp = rest[pages_per_step:2 * pages_per_step]
    o_ref = rest[2 * pages_per_step]
    m_sc, l_sc, a_sc = rest[2 * pages_per_step + 1:]
    j = pl.program_id(1)
    nrow = DA_HEADS * 2 * n_tok
    ncol = PAGE_SIZE * DA_HEADS

    @pl.when(j == 0)
    def _():
        m_sc[...] = jnp.full(m_sc.shape, NEG_BIG, F32)
        l_sc[...] = jnp.zeros(l_sc.shape, F32)
        a_sc[...] = jnp.zeros(a_sc.shape, F32)

    q = q_ref[0]
    row = lax.broadcasted_iota(jnp.int32, (nrow, ncol), 0)
    colk = lax.broadcasted_iota(jnp.int32, (nrow, ncol), 1)
    same_head = (colk % DA_HEADS) == (row // (2 * n_tok))

    def update(s_list, v_list):
        m = m_sc[...]
        mn = m
        for s in s_list:
            mn = jnp.maximum(mn, jnp.max(s, axis=-1, keepdims=True))
        alpha = jnp.exp2(m - mn)
        l = alpha * l_sc[...]
        a = alpha * a_sc[...]
        for s, v in zip(s_list, v_list):
            p = jnp.exp2(s - mn)
            l = l + jnp.sum(p, axis=-1, keepdims=True)
            a = a + jnp.dot(p.astype(BF16), v, preferred_element_type=F32)
        m_sc[...] = mn
        l_sc[...] = l
        a_sc[...] = a

    s_list, v_list = [], []
    for i in range(pages_per_step):
        kb = kp[i][0, 0].astype(BF16)
        s_list.append(jnp.where(same_head, _nt_dot(q, kb), NEG_BIG))
        v_list.append(vp[i][0, 0].astype(BF16))
    update(s_list, v_list)

    @pl.when(j == pl.num_programs(1) - 1)
    def _():
        rown = lax.broadcasted_iota(jnp.int32, (nrow, kn_ref.shape[1]), 0)
        coln = lax.broadcasted_iota(jnp.int32, (nrow, kn_ref.shape[1]), 1)
        ok = ((coln < n_tok * DA_HEADS) & ((coln % DA_HEADS) == (rown // (2 * n_tok)))
              & ((coln // DA_HEADS) <= (rown % n_tok)))
        s = jnp.where(ok, _nt_dot(q, kn_ref[0]), NEG_BIG)
        update([s], [vn_ref[0]])
        a = a_sc[...] / l_sc[...]
        lam = _diff_lambda(lq1_ref[...], lk1_ref[...], lq2_ref[...], lk2_ref[...], lam_init)
        for h in range(DA_HEADS):
            base = h * 2 * n_tok
            o = a[base:base + n_tok] - lam * a[base + n_tok:base + 2 * n_tok]
            o = _rmsnorm_rows(o, sg_ref[...], SUBLN_EPS) * (1.0 - lam_init)
            o_ref[0, :, h * LANES:(h + 1) * LANES] = o


def _attn_sample(layer, q_rows, kn, vn, cache_k, cache_v, page_table, lams, sg, lam_init,
                 n_tok, pages_per_step):
    db, nrow, _ = q_rows.shape
    n_pages = page_table.shape[1]
    assert n_pages % pages_per_step == 0
    pt = page_table.reshape(-1)
    small = pl.BlockSpec((1, DA_HEAD_DIM), lambda b, j, p: (0, 0))
    per_seq = lambda b, j, p: (b, 0, 0)

    def page_spec(i):
        return pl.BlockSpec(
            (1, 1, PAGE_SIZE * DA_HEADS, LANES),
            lambda b, j, p: (layer, p[b * n_pages + j * pages_per_step + i], 0, 0))

    pages = [page_spec(i) for i in range(pages_per_step)]
    grid_spec = pltpu.PrefetchScalarGridSpec(
        num_scalar_prefetch=1,
        grid=(db, n_pages // pages_per_step),
        in_specs=[small, small, small, small,
                  pl.BlockSpec((1, DA_V_DIM), lambda b, j, p: (0, 0)),
                  pl.BlockSpec((1, nrow, LANES), per_seq),
                  pl.BlockSpec((1,) + kn.shape[1:], per_seq),
                  pl.BlockSpec((1,) + vn.shape[1:], per_seq)] + pages + pages,
        out_specs=pl.BlockSpec((1, n_tok, DA_WIDTH), per_seq),
        scratch_shapes=[pltpu.VMEM((nrow, 1), F32), pltpu.VMEM((nrow, 1), F32),
                        pltpu.VMEM((nrow, DA_V_DIM), F32)],
    )
    return pl.pallas_call(
        functools.partial(_attn_sample_body, lam_init, n_tok, pages_per_step),
        grid_spec=grid_spec,
        out_shape=jax.ShapeDtypeStruct((db, n_tok, DA_WIDTH), F32),
        compiler_params=_cparams(("parallel", "arbitrary")),
        name="attn_sample",
    )(pt, *lams, sg, q_rows, kn, vn, *([cache_k] * pages_per_step), *([cache_v] * pages_per_step))


def _mix_out_prompt_body(n_chunk, a_ref, u_ref, gv_ref, h_ref, ws_ref, bst_ref, wo_ref, gf_ref,
                         ho_ref, xn_ref, g_sc):
    r = lax.broadcasted_iota(jnp.int32, (CHUNK, CHUNK), 0)
    c = lax.broadcasted_iota(jnp.int32, (CHUNK, CHUNK), 1)
    for g in range(GM_GROUPS):
        w = jnp.where(c <= r, ws_ref[g], 0.0).astype(BF16)
        bias = bst_ref[:, g:g + 1]
        lanes = slice(g * GM_CH, (g + 1) * GM_CH)
        for ci in range(n_chunk):
            rows = slice(ci * CHUNK, (ci + 1) * CHUNK)
            m = jnp.dot(w, gv_ref[rows, lanes], preferred_element_type=F32) + bias
            g_sc[rows, lanes] = (u_ref[rows, lanes].astype(F32) * m).astype(BF16)
    out = (jnp.dot(a_ref[...], wo_ref[:DA_WIDTH, :], preferred_element_type=F32)
           + jnp.dot(g_sc[...], wo_ref[DA_WIDTH:, :], preferred_element_type=F32))
    hn = h_ref[...] + out
    ho_ref[...] = hn
    _put_rows(xn_ref, _rmsnorm_rows(hn, gf_ref[...], RMS_EPS).astype(xn_ref.dtype))


def _mix_out_sample_body(n_tok, n_seq, a_ref, u_ref, gv_ref, h_ref, ws_ref, bs_ref, wo_ref, gf_ref,
                         ho_ref, xn_ref, g_sc):
    for g in range(GM_GROUPS):
        lanes = slice(g * GM_CH, (g + 1) * GM_CH)
        for t in range(n_tok):
            m = jnp.zeros((n_seq, GM_CH), F32) + bs_ref[g:g + 1, t:t + 1]
            for s in range(t + 1):
                m = m + ws_ref[g, t:t + 1, s:s + 1] * gv_ref[s * n_seq:(s + 1) * n_seq, lanes]
            rows = slice(t * n_seq, (t + 1) * n_seq)
            g_sc[rows, lanes] = u_ref[rows, lanes] * m
    out = (jnp.dot(a_ref[...], wo_ref[:DA_WIDTH, :], preferred_element_type=F32)
           + jnp.dot(g_sc[...].astype(BF16), wo_ref[DA_WIDTH:, :], preferred_element_type=F32))
    hn = h_ref[...] + out
    ho_ref[...] = hn
    _put_rows(xn_ref, _rmsnorm_rows(hn, gf_ref[...], RMS_EPS).astype(xn_ref.dtype))


def _mix_out(a, u, gv, h, ws, bs, wo_bf, gf, tm, sample_shape=None, tile_rows=False):
    m, d = h.shape
    row = lambda i: (i, 0)
    tpr = _tiles_per_row(d) if tile_rows else 1
    fixed2 = lambda i: (0, 0)
    fixed3 = lambda i: (0, 0, 0)
    if sample_shape is None:
        body = functools.partial(_mix_out_prompt_body, tm // CHUNK)
        bias = bs.T
        g_dtype = BF16
    else:
        body = functools.partial(_mix_out_sample_body, *sample_shape)
        bias = bs
        g_dtype = F32
    return pl.pallas_call(
        body,
        grid=(m // tm,),
        in_specs=[pl.BlockSpec((tm, DA_WIDTH), row),
                  pl.BlockSpec((tm, GM_WIDTH), row),
                  pl.BlockSpec((tm, GM_WIDTH), row),
                  pl.BlockSpec((tm, d), row),
                  pl.BlockSpec(ws.shape, fixed3),
                  pl.BlockSpec(bias.shape, fixed2),
                  pl.BlockSpec(wo_bf.shape, fixed2),
                  pl.BlockSpec((1, d), fixed2)],
        out_specs=[pl.BlockSpec((tm, d), row), pl.BlockSpec((tm * tpr, d // tpr), row)],
        out_shape=[jax.ShapeDtypeStruct((m, d), F32),
                   jax.ShapeDtypeStruct((m * tpr, d // tpr), F32 if tile_rows else BF16)],
        scratch_shapes=[pltpu.VMEM((tm, GM_WIDTH), g_dtype)],
        compiler_params=_cparams(("parallel",)),
        name="mix_out",
    )(a, u, gv, h, ws, bias, wo_bf, gf)


def _swiglu(x, wg_ref, wu_ref, wd_ref):
    g = jnp.dot(x, wg_ref[...], preferred_element_type=F32)
    u = jnp.dot(x, wu_ref[...], preferred_element_type=F32)
    hid = (g * jax.nn.sigmoid(g) * u).astype(BF16)
    return jnp.dot(hid, wd_ref[...], preferred_element_type=F32)


def _ffn_body(xn_ref, h_ref, wg_ref, wu_ref, wd_ref, o_ref):
    o_ref[...] = h_ref[...] + _swiglu(xn_ref[...], wg_ref, wu_ref, wd_ref)


def _ffn(xn, h, wg, wu, wd, tm):
    m, d = h.shape
    ff = wg.shape[1]
    resident = dict(pipeline_mode=pl.Buffered(1))
    return pl.pallas_call(
        _ffn_body,
        grid=(m // tm,),
        in_specs=[pl.BlockSpec((tm, d), lambda i: (i, 0)),
                  pl.BlockSpec((tm, d), lambda i: (i, 0)),
                  pl.BlockSpec((d, ff), lambda i: (0, 0), **resident),
                  pl.BlockSpec((d, ff), lambda i: (0, 0), **resident),
                  pl.BlockSpec((ff, d), lambda i: (0, 0), **resident)],
        out_specs=pl.BlockSpec((tm, d), lambda i: (i, 0)),
        out_shape=jax.ShapeDtypeStruct((m, d), F32),
        compiler_params=_cparams(("parallel",)),
        name="ffn",
    )(xn, h, wg, wu, wd)


R_E1, R_E2, R_G1, R_G2, R_RANK1, R_RANK2 = range(6)


def _router_body(n_exp, xn_ref, wr_ref, base_ref, o_ref, cnt_ref, cnt_sc):
    i = pl.program_id(0)

    @pl.when(i == 0)
    def _():
        cnt_sc[...] = base_ref[...]

    d = wr_ref.shape[0]
    tm = xn_ref.shape[0] // _tiles_per_row(d)
    logits = jnp.dot(_get_rows(xn_ref, tm, d), wr_ref[...], preferred_element_type=F32,
                     precision=lax.Precision.HIGHEST)
    col = lax.broadcasted_iota(jnp.int32, logits.shape, 1)
    logits = jnp.where(col < n_exp, logits, -jnp.inf)
    big = jnp.int32(LANES)
    v1 = jnp.max(logits, axis=-1, keepdims=True)
    i1 = jnp.min(jnp.where(logits == v1, col, big), axis=-1, keepdims=True)
    rest = jnp.where(col == i1, -jnp.inf, logits)
    v2 = jnp.max(rest, axis=-1, keepdims=True)
    i2 = jnp.min(jnp.where(rest == v2, col, big), axis=-1, keepdims=True)
    e = jnp.exp(v2 - v1)
    g1 = 1.0 / (1.0 + e)
    g2 = e / (1.0 + e)
    hit = (col == i1) | (col == i2)
    r = lax.broadcasted_iota(jnp.int32, (tm, tm), 0)
    c = lax.broadcasted_iota(jnp.int32, (tm, tm), 1)
    below = jnp.where(c < r, 1.0, 0.0).astype(BF16)
    prefix = jnp.dot(below, jnp.where(hit, 1.0, 0.0).astype(BF16), preferred_element_type=F32)
    rank = prefix + cnt_sc[...]
    r1 = jnp.sum(jnp.where(col == i1, rank, 0.0), axis=-1, keepdims=True)
    r2 = jnp.sum(jnp.where(col == i2, rank, 0.0), axis=-1, keepdims=True)
    cnt_sc[...] += jnp.sum(jnp.where(hit, 1.0, 0.0), axis=0, keepdims=True)
    rec = jnp.zeros(logits.shape, F32)
    for idx, val in ((R_E1, i1.astype(F32)), (R_E2, i2.astype(F32)), (R_G1, g1), (R_G2, g2),
                     (R_RANK1, r1), (R_RANK2, r2)):
        rec = jnp.where(col == idx, val, rec)
    o_ref[...] = rec
    cnt_ref[...] = cnt_sc[...]


def _router(xn_t, wr_pad, base_counts, n_exp, tm):
    tpr = _tiles_per_row(wr_pad.shape[0])
    m = xn_t.shape[0] // tpr
    return pl.pallas_call(
        functools.partial(_router_body, n_exp),
        grid=(m // tm,),
        in_specs=[pl.BlockSpec((tm * tpr, LANES), lambda i: (i, 0)),
                  pl.BlockSpec(wr_pad.shape, lambda i: (0, 0)),
                  pl.BlockSpec((1, LANES), lambda i: (0, 0))],
        out_specs=[pl.BlockSpec((tm, LANES), lambda i: (i, 0)),
                   pl.BlockSpec((1, LANES), lambda i: (0, 0))],
        out_shape=[jax.ShapeDtypeStruct((m, LANES), F32), jax.ShapeDtypeStruct((1, LANES), F32)],
        scratch_shapes=[pltpu.VMEM((1, LANES), F32)],
        compiler_params=_cparams(("arbitrary",)),
        name="router",
    )(xn_t, wr_pad, base_counts)


def _tile_copy(src, src_row, dst, dst_row, tpr, sem):
    return pltpu.make_async_copy(src.at[pl.ds(pl.multiple_of(src_row * tpr, tpr), tpr), :],
                                 dst.at[pl.ds(pl.multiple_of(dst_row * tpr, tpr), tpr), :], sem)


def _dispatch_body(tm, tpr, slot_ref, xn_ref, xs_in_ref, xs_ref, sem):
    del xs_in_ref

    def start(t, c):
        for k in range(TOP_K):
            _tile_copy(xn_ref, t, xs_ref, slot_ref[0, 0, TOP_K * t + k], tpr, sem).start(
                priority=k % 2)
        return c

    def wait(t, c):
        for k in range(TOP_K):
            _tile_copy(xn_ref, t, xs_ref, t, tpr, sem).wait()
        return c

    lax.fori_loop(0, tm, start, 0, unroll=DMA_UNROLL)
    lax.fori_loop(0, tm, wait, 0, unroll=DMA_UNROLL)


def _dispatch(slots, xn_t, xs_t, tm, tpr):
    m = xn_t.shape[0] // tpr
    slots3 = slots.reshape(m // tm, 1, tm * TOP_K)
    return pl.pallas_call(
        functools.partial(_dispatch_body, tm, tpr),
        grid=(m // tm,),
        in_specs=[pl.BlockSpec((1, 1, tm * TOP_K), lambda i: (i, 0, 0), memory_space=pltpu.SMEM),
                  pl.BlockSpec((tm * tpr, LANES), lambda i: (i, 0)),
                  pl.BlockSpec(memory_space=pl.ANY)],
        out_specs=pl.BlockSpec(memory_space=pl.ANY),
        out_shape=jax.ShapeDtypeStruct(xs_t.shape, xs_t.dtype),
        scratch_shapes=[pltpu.SemaphoreType.DMA(())],
        input_output_aliases={2: 0},
        compiler_params=_cparams(("arbitrary",)),
        name="moe_dispatch",
    )(slots3, xn_t, xs_t)


def _combine_body(tm, tpr, slot_cur_ref, slot_nxt_ref, rec_ref, h_ref, g_ref, ys_ref, o_ref,
                  buf, sem):
    i = pl.program_id(0)
    n = pl.num_programs(0)
    d = h_ref.shape[1]

    def fetch(slot_ref, half):
        def start(t, c):
            for k in range(TOP_K):
                _tile_copy(ys_ref, slot_ref[0, 0, TOP_K * t + k], buf.at[half, k], t, tpr,
                           sem.at[half]).start(priority=k % 2)
            return c
        lax.fori_loop(0, tm, start, 0, unroll=DMA_UNROLL)

    def wait_block(half):
        def wait(t, c):
            for k in range(TOP_K):
                _tile_copy(ys_ref, t, buf.at[half, k], t, tpr, sem.at[half]).wait()
            return c
        lax.fori_loop(0, tm, wait, 0, unroll=DMA_UNROLL)

    def step(half):
        @pl.when(i == 0)
        def _():
            fetch(slot_cur_ref, half)

        @pl.when(i + 1 < n)
        def _():
            fetch(slot_nxt_ref, 1 - half)

        wait_block(half)
        rec = rec_ref[...]
        y = (rec[:, R_G1:R_G1 + 1] * _get_rows(buf.at[half, 0], tm, d)
             + rec[:, R_G2:R_G2 + 1] * _get_rows(buf.at[half, 1], tm, d))
        o_ref[...] = _rmsnorm_rows(h_ref[...] + y, g_ref[...], RMS_EPS)

    @pl.when(i % 2 == 0)
    def _():
        step(0)

    @pl.when(i % 2 == 1)
    def _():
        step(1)


def _combine_norm(slots, rec, h, g, ys_t, tm):
    m, d = h.shape
    tpr = _tiles_per_row(d)
    n = m // tm
    slots3 = slots.reshape(n, 1, tm * TOP_K)
    row = lambda i: (i, 0)
    return pl.pallas_call(
        functools.partial(_combine_body, tm, tpr),
        grid=(n,),
        in_specs=[pl.BlockSpec((1, 1, tm * TOP_K), lambda i: (i, 0, 0), memory_space=pltpu.SMEM),
                  pl.BlockSpec((1, 1, tm * TOP_K), lambda i: (jnp.minimum(i + 1, n - 1), 0, 0),
                               memory_space=pltpu.SMEM),
                  pl.BlockSpec((tm, LANES), row),
                  pl.BlockSpec((tm, d), row),
                  pl.BlockSpec((1, d), lambda i: (0, 0)),
                  pl.BlockSpec(memory_space=pl.ANY)],
        out_specs=pl.BlockSpec((tm, d), row),
        out_shape=jax.ShapeDtypeStruct((m, d), F32),
        scratch_shapes=[pltpu.VMEM((2, TOP_K, tm * tpr, LANES), F32),
                        pltpu.SemaphoreType.DMA((2,))],
        compiler_params=_cparams(("arbitrary",)),
        name="moe_combine_norm",
    )(slots3, slots3, rec, h, g, ys_t)


def _moe_body(tm, be_ref, nv_ref, xs_ref, wg_ref, wu_ref, wd_ref, o_ref, x_sc, acc_sc):
    i = pl.program_id(0)
    f = pl.program_id(1)
    last = pl.num_programs(1) - 1
    live = i < nv_ref[0]
    d = x_sc.shape[1]

    @pl.when(live & (f == 0))
    def _():
        acc_sc[...] = jnp.zeros(acc_sc.shape, F32)
        x_sc[...] = _get_rows(xs_ref, tm, d).astype(BF16)

    @pl.when(live)
    def _():
        acc_sc[...] += _swiglu(x_sc[...], wg_ref.at[0], wu_ref.at[0], wd_ref.at[0])

    @pl.when(live & (f == last))
    def _():
        _put_rows(o_ref, acc_sc[...])

    @pl.when(jnp.logical_not(live) & (f == last))
    def _():
        o_ref[...] = jnp.zeros(o_ref.shape, F32)


def _moe_experts(xs_t, block_expert, n_valid, wg, wu, wd, tm, tf):
    d, ff = wg.shape[1], wg.shape[2]
    tpr = _tiles_per_row(d)
    p = xs_t.shape[0] // tpr
    rows = pl.BlockSpec((tm * tpr, LANES), lambda i, f, be, nv: (i, 0))
    grid_spec = pltpu.PrefetchScalarGridSpec(
        num_scalar_prefetch=2,
        grid=(p // tm, ff // tf),
        in_specs=[rows,
                  pl.BlockSpec((1, d, tf), lambda i, f, be, nv: (be[i], 0, f)),
                  pl.BlockSpec((1, d, tf), lambda i, f, be, nv: (be[i], 0, f)),
                  pl.BlockSpec((1, tf, d), lambda i, f, be, nv: (be[i], f, 0))],
        out_specs=rows,
        scratch_shapes=[pltpu.VMEM((tm, d), BF16), pltpu.VMEM((tm, d), F32)],
    )
    return pl.pallas_call(
        functools.partial(_moe_body, tm),
        grid_spec=grid_spec,
        out_shape=jax.ShapeDtypeStruct(xs_t.shape, F32),
        compiler_params=_cparams(("parallel", "arbitrary")),
        name="moe_experts",
    )(block_expert, n_valid, xs_t, wg, wu, wd)


def _moe_final(groups, w_router, wg, wu, wd, final_g, tm, tf):
    d = w_router.shape[0]
    n_exp = w_router.shape[1]
    n_tok = sum(h.shape[0] for _, h, _ in groups)
    wr_pad = jnp.zeros((d, LANES), F32).at[:, :n_exp].set(w_router)
    counts = jnp.zeros((1, LANES), F32)
    recs = []
    for xn, h, _ in groups:
        rec, counts = _router(xn, wr_pad, counts, n_exp, _pick(h.shape[0], 512))
        recs.append(rec)
    cnt = counts[0, :n_exp].astype(jnp.int32)
    padded = ((cnt + tm - 1) // tm) * tm
    group_end = jnp.cumsum(padded)
    group_start = group_end - padded
    n_slots = ((n_tok * TOP_K + n_exp * (tm - 1)) // tm) * tm
    n_blocks = n_slots // tm
    block_start = jnp.arange(n_blocks, dtype=jnp.int32) * tm
    n_valid = (group_end[-1] // tm).astype(jnp.int32)
    block_expert = jnp.sum((block_start[:, None] >= group_end[None, :]).astype(jnp.int32), axis=1)
    last_valid = jnp.sum((jnp.maximum(n_valid - 1, 0) * tm >= group_end).astype(jnp.int32))
    block_expert = jnp.minimum(jnp.where(block_start < n_valid * tm, block_expert, last_valid),
                               n_exp - 1).astype(jnp.int32)

    def slots_of(rec):
        e = rec[:, R_E1:R_E2 + 1].astype(jnp.int32)
        rank = rec[:, R_RANK1:R_RANK2 + 1].astype(jnp.int32)
        start = jnp.sum(jnp.where(e[..., None] == jnp.arange(n_exp), group_start, 0), axis=-1)
        return (start + rank).astype(jnp.int32)

    slots = [slots_of(rec) for rec in recs]
    tpr = _tiles_per_row(d)
    xs = jnp.zeros((n_slots * tpr, LANES), F32)
    for (xn, _, tb), sl in zip(groups, slots):
        xs = _dispatch(sl, xn, xs, _pick(sl.shape[0], 4 * tb), tpr)
    ys = _moe_experts(xs, block_expert, n_valid.reshape(1), wg, wu, wd, tm, tf)
    return [_combine_norm(sl, rec, h, final_g, ys, tb)
            for (_, h, tb), sl, rec in zip(groups, slots, recs)]


def kernel(x_prompt, x_sample, cache_k, cache_v, page_table, w_in, w_out, norm_mix_g, norm_ffn_g,
           lam_q1, lam_k1, lam_q2, lam_k2, subln_g, gm_ln_g, gm_ln_b, gm_ws, gm_bs,
           ffn_wg, ffn_wu, ffn_wd, moe_router, moe_wg, moe_wu, moe_wd, final_norm_g):
    b, s, d = x_prompt.shape
    db, t = x_sample.shape[:2]
    depth = w_in.shape[0]
    n_pages = page_table.shape[1]
    past = n_pages * PAGE_SIZE
    n_p, n_s = b * s, db * t
    assert depth == 2 and s % CHUNK == 0

    plan = _tile_plan(s, n_p, n_pages, moe_wg.shape[-1])
    tm_p, blk, pages_per_step = plan["rows"], plan["attn_blk"], plan["pages_per_step"]
    to_bf16 = functools.partial(_to_bf16, rows=plan["cast_rows"])
    w_in_bf, w_out_bf = to_bf16(w_in), to_bf16(w_out)

    hp = x_prompt.reshape(n_p, d)
    hs = x_sample.transpose(1, 0, 2).reshape(n_s, d)
    tab_p = _rope_tables(jnp.arange(s, dtype=F32))
    pos_s = jnp.arange(t, dtype=F32) + jnp.float32(past)
    tab_s = _rope_tables(jnp.repeat(pos_s, db))
    ck = cache_k.reshape(cache_k.shape[0], cache_k.shape[1], PAGE_SIZE * DA_HEADS, LANES)
    cv = cache_v.reshape(cache_v.shape[0], cache_v.shape[1], PAGE_SIZE * DA_HEADS, LANES)
    lane = jnp.arange(LANES)
    comp_mask = jnp.stack([lane < DA_HEAD_DIM, lane >= DA_HEAD_DIM])
    new_rows = PAGE_SIZE
    assert t * DA_HEADS <= new_rows

    outs = {k: [] for k in ("ks", "vs", "gv")}
    y_p = y_s = k_all = v_all = None
    for l in range(depth):
        lam_init = 0.8 - 0.6 * math.exp(-0.3 * l)
        g_mix = norm_mix_g[l][None]
        g_ffn = norm_ffn_g[l][None]
        lng, lnb = gm_ln_g[l][None], gm_ln_b[l][None]
        lams = (lam_q1[l][None], lam_k1[l][None], lam_q2[l][None], lam_k2[l][None])
        sg = subln_g[l][None]

        q, k_all, v_all, kb, vb, u, gv = _proj(hp, g_mix, w_in_bf[l], tab_p, lng, lnb, tm_p, BF16,
                                               BF16, kv_stack=(l, depth, k_all, v_all))
        a = _attn_prompt(q.reshape(b, s, QK_COLS), kb.reshape(b, s, QK_COLS),
                         vb.reshape(b, s, 2 * DA_WIDTH), lams, sg, lam_init, blk)
        moe_layer = l % 2 == 1
        hp, xn_p = _mix_out(a.reshape(n_p, DA_WIDTH), u, gv, hp, gm_ws[l], gm_bs[l], w_out_bf[l],
                            g_ffn, tm_p, tile_rows=moe_layer)

        sq, sk, sv, skb, svb, su, sgv = _proj(hs, g_mix, w_in_bf[l], tab_s, lng, lnb, n_s, F32, F32)
        q5 = sq.reshape(t, db, DA_HEADS, 1, LANES).transpose(1, 2, 3, 0, 4)
        q_rows = jnp.where(comp_mask[None, None, :, None, :], q5, jnp.zeros((), BF16))
        q_rows = q_rows.reshape(db, DA_HEADS * 2 * t, LANES)
        pad = ((0, 0), (0, new_rows - t * DA_HEADS), (0, 0))
        kn = jnp.pad(skb.reshape(t, db, DA_HEADS, LANES).transpose(1, 0, 2, 3).reshape(db, t * DA_HEADS, LANES), pad)
        svb = svb.reshape(t, db, DA_HEADS, 2 * DA_V_DIM)[..., :DA_V_DIM]
        vn = jnp.pad(svb.transpose(1, 0, 2, 3).reshape(db, t * DA_HEADS, LANES), pad)
        sa = _attn_sample(l, q_rows, kn, vn, ck, cv, page_table, lams, sg, lam_init, t, pages_per_step)
        sa = sa.transpose(1, 0, 2).reshape(n_s, DA_WIDTH).astype(BF16)
        hs, xn_s = _mix_out(sa, su, sgv, hs, gm_ws[l], gm_bs[l], w_out_bf[l], g_ffn, n_s,
                            sample_shape=(t, db), tile_rows=moe_layer)

        outs["ks"].append(sk.reshape(t, db, DA_HEADS, 2 * DA_HEAD_DIM).transpose(1, 0, 2, 3))
        outs["vs"].append(sv.reshape(t, db, DA_HEADS, DA_V_DIM).transpose(1, 0, 2, 3))
        outs["gv"].append(sgv.reshape(t, db, GM_GROUPS, GM_CH).transpose(1, 0, 2, 3))

        j = l // 2
        if l % 2 == 0:
            wg, wu, wd = to_bf16(ffn_wg[j]), to_bf16(ffn_wu[j]), to_bf16(ffn_wd[j])
            hp = _ffn(xn_p, hp, wg, wu, wd, tm_p)
            hs = _ffn(xn_s, hs, wg, wu, wd, n_s)
        else:
            wg, wu, wd = to_bf16(moe_wg[j]), to_bf16(moe_wu[j]), to_bf16(moe_wd[j])
            y_p, y_s = _moe_final([(xn_p, hp, plan["combine_rows"]), (xn_s, hs, n_s)],
                                  moe_router[j], wg, wu, wd, final_norm_g[None],
                                  plan["expert_rows"], plan["expert_cols"])

    y_prompt = y_p.reshape(b, s, d)
    y_sample = y_s.reshape(t, db, d).transpose(1, 0, 2)
    return (y_prompt, y_sample,
            k_all.reshape(depth, b, s, DA_HEADS, 2 * DA_HEAD_DIM),
            v_all.reshape(depth, b, s, DA_HEADS, DA_V_DIM),
            jnp.stack(outs["ks"]), jnp.stack(outs["vs"]), jnp.stack(outs["gv"]))
```

```python
import functools
import math

import jax
import jax.numpy as jnp
from jax import lax
from jax.experimental import pallas as pl
from jax.experimental.pallas import tpu as pltpu

F32 = jnp.float32
BF16 = jnp.bfloat16

DA_HEADS = 4
DA_HEAD_DIM = 64
DA_V_DIM = 2 * DA_HEAD_DIM
QK_COLS = DA_HEADS * 2 * DA_HEAD_DIM
DA_WIDTH = DA_HEADS * DA_V_DIM
ROT_DIM = DA_HEAD_DIM // 4
ROPE_THETA = 500000.0
GM_GROUPS = 4
GM_CH = 128
GM_WIDTH = GM_GROUPS * GM_CH
CHUNK = 128
PAGE_SIZE = 128
TOP_K = 2
RMS_EPS = 1e-6
SUBLN_EPS = 1e-5
LN_EPS = 1e-5
NEG_BIG = -1e30
LOG2_E = math.log2(math.e)
LANES = 128
V7X_VMEM_BYTES = 64 * 1024 * 1024
VMEM_LIMIT = V7X_VMEM_BYTES * 13 // 16
DMA_UNROLL = 8


def _pick(total, pref):
    t = min(total, pref)
    while total % t:
        t //= 2
    return t


def _tile_plan(s, n_p, n_pages):
    rows = _pick(s, 512)
    assert rows % CHUNK == 0
    return dict(
        rows=rows,
        attn_blk=_pick(s, 512),
        pages_per_step=_pick(n_pages, 32),
        expert_rows=512,
        combine_rows=_pick(n_p, 256),
    )


def _cparams(sem):
    return pltpu.CompilerParams(dimension_semantics=sem, vmem_limit_bytes=VMEM_LIMIT)


def _to_bf16(w):
    return w.astype(BF16)


def _nt_dot(a, b):
    return lax.dot_general(a, b, (((1,), (1,)), ((), ())), preferred_element_type=F32)


def _rmsnorm_rows(x, g, eps):
    return x * lax.rsqrt(jnp.mean(x * x, axis=-1, keepdims=True) + eps) * g


def _tiles_per_row(d):
    return d // LANES


def _put_rows(ref, x):
    n, d = x.shape
    if ref.shape == x.shape:
        ref[...] = x
        return
    tpr = _tiles_per_row(d)
    assert ref.shape == (n * tpr, LANES)
    for c in range(tpr):
        ref[pl.ds(c, n, stride=tpr), :] = x[:, c * LANES:(c + 1) * LANES]


def _get_rows(ref, n, d):
    tpr = _tiles_per_row(d)
    return jnp.concatenate([ref[pl.ds(c, n, stride=tpr), :] for c in range(tpr)], axis=1)


def _gelu_exact(x):
    return 0.5 * x * (1.0 + lax.erf(x * (2.0 ** -0.5)))


def _diff_lambda(lq1, lk1, lq2, lk2, lam_init):
    a = jnp.exp(jnp.sum(lq1 * lk1, axis=-1, keepdims=True))
    b = jnp.exp(jnp.sum(lq2 * lk2, axis=-1, keepdims=True))
    return a - b + lam_init


def _put_head(ref, t, x):
    if ref.shape[-1] == DA_HEADS * LANES:
        ref[:, t * LANES:(t + 1) * LANES] = x
    else:
        for j in range(ref.shape[0]):
            ref.at[j][pl.ds(t, x.shape[0], stride=DA_HEADS), :] = x


def _proj_body(n_alias, h_ref, g_ref, w_ref, cos_ref, sa_ref, sb_ref, lng_ref, lnb_ref, *rest):
    q_ref, k_ref, v_ref, kb_ref, vb_ref, u_ref, gv_ref = rest[n_alias:]
    h = h_ref[...]
    xn = _rmsnorm_rows(h, g_ref[...], RMS_EPS).astype(BF16)
    z = jnp.dot(xn, w_ref[...], preferred_element_type=F32)
    cos, sa, sb = cos_ref[...], sa_ref[...], sb_ref[...]

    def cols(lo, width):
        return z[:, lo:lo + width]

    def rope(x):
        return (x * cos + pltpu.roll(x, ROT_DIM // 2, 1) * sa
                + pltpu.roll(x, LANES - ROT_DIM // 2, 1) * sb)

    scale = DA_HEAD_DIM ** -0.5 * LOG2_E
    ones = jnp.ones((h.shape[0], DA_V_DIM), vb_ref.dtype)
    zq = cols(0, QK_COLS)
    for t in range(DA_HEADS):
        sl = slice(t * LANES, (t + 1) * LANES)
        q_ref[:, sl] = (rope(zq[:, sl]) * scale).astype(q_ref.dtype)
    zk = cols(QK_COLS, QK_COLS)
    for t in range(DA_HEADS):
        sl = slice(t * LANES, (t + 1) * LANES)
        kt = rope(zk[:, sl])
        _put_head(k_ref, t, kt)
        kb_ref[:, sl] = kt.astype(kb_ref.dtype)
    zv = cols(2 * QK_COLS, DA_WIDTH)
    for t in range(DA_HEADS):
        vt = zv[:, t * DA_V_DIM:(t + 1) * DA_V_DIM]
        _put_head(v_ref, t, vt)
        vb_ref[:, 2 * t * DA_V_DIM:(2 * t + 1) * DA_V_DIM] = vt.astype(vb_ref.dtype)
        vb_ref[:, (2 * t + 1) * DA_V_DIM:(2 * t + 2) * DA_V_DIM] = ones
    u_ref[...] = _gelu_exact(cols(2 * QK_COLS + DA_WIDTH, GM_WIDTH)).astype(u_ref.dtype)
    gvr = _gelu_exact(cols(2 * QK_COLS + DA_WIDTH + GM_WIDTH, GM_WIDTH))
    mu = jnp.mean(gvr, axis=-1, keepdims=True)
    d = gvr - mu
    var = jnp.mean(d * d, axis=-1, keepdims=True)
    gv = d * lax.rsqrt(var + LN_EPS) * lng_ref[...] + lnb_ref[...]
    gv_ref[...] = gv.astype(gv_ref.dtype)


def _proj(h, g, w_bf, tables, lng, lnb, tm, u_dtype, gv_dtype, kv_stack=None):
    m, d = h.shape
    cos_t, sa_t, sb_t = tables
    ntab = cos_t.shape[0] // tm
    row = lambda i: (i, 0)
    fixed = lambda i: (0, 0)
    tab = lambda i: (i % ntab, 0)
    wide = lambda dt: jax.ShapeDtypeStruct((m, QK_COLS), dt)
    in_specs = [
        pl.BlockSpec((tm, d), row),
        pl.BlockSpec((1, d), fixed),
        pl.BlockSpec(w_bf.shape, fixed),
        pl.BlockSpec((tm, LANES), tab),
        pl.BlockSpec((tm, LANES), tab),
        pl.BlockSpec((tm, LANES), tab),
        pl.BlockSpec((1, GM_WIDTH), fixed),
        pl.BlockSpec((1, GM_WIDTH), fixed),
    ]
    args = [h, g, w_bf, cos_t, sa_t, sb_t, lng, lnb]
    kv_spec = pl.BlockSpec((tm, QK_COLS), row)
    kv_shape = wide(F32)
    aliases = {}
    if kv_stack is not None:
        layer, depth, k_all, v_all = kv_stack
        kv_shape = jax.ShapeDtypeStruct((depth, m * DA_HEADS, LANES), F32)
        if k_all is None:
            assert layer == 0
            kv_spec = pl.BlockSpec((depth, tm * DA_HEADS, LANES), lambda i: (0, i, 0))
        else:
            kv_spec = pl.BlockSpec((1, tm * DA_HEADS, LANES), lambda i: (layer, i, 0))
            aliases = {len(args): 1, len(args) + 1: 2}
            in_specs += [pl.BlockSpec(memory_space=pl.ANY)] * 2
            args += [k_all, v_all]
    return pl.pallas_call(
        functools.partial(_proj_body, len(aliases)),
        grid=(m // tm,),
        in_specs=in_specs,
        out_specs=[pl.BlockSpec((tm, QK_COLS), row), kv_spec, kv_spec,
                   pl.BlockSpec((tm, QK_COLS), row), pl.BlockSpec((tm, 2 * DA_WIDTH), row),
                   pl.BlockSpec((tm, GM_WIDTH), row), pl.BlockSpec((tm, GM_WIDTH), row)],
        out_shape=[wide(BF16), kv_shape, kv_shape, wide(BF16),
                   jax.ShapeDtypeStruct((m, 2 * DA_WIDTH), BF16), wide(u_dtype), wide(gv_dtype)],
        input_output_aliases=aliases,
        compiler_params=_cparams(("parallel",)),
        name="proj",
    )(*args)


def _rope_tables(pos):
    inv = ROPE_THETA ** (-jnp.arange(0, ROT_DIM, 2, dtype=F32) / ROT_DIM)
    ang = pos[:, None] * inv[None, :]
    c, s = jnp.cos(ang), jnp.sin(ang)
    j = jnp.arange(LANES) % DA_HEAD_DIM
    first = j < ROT_DIM // 2
    second = (j >= ROT_DIM // 2) & (j < ROT_DIM)
    idx = j % (ROT_DIM // 2)
    cg, sg = c[:, idx], s[:, idx]
    cos_t = jnp.where(first | second, cg, 1.0)
    sa_t = jnp.where(second, sg, 0.0)
    sb_t = jnp.where(first, -sg, 0.0)
    return cos_t, sa_t, sb_t


def _attn_prompt_body(lam_init, blk, lq1_ref, lk1_ref, lq2_ref, lk2_ref, sg_ref,
                      q_ref, k_ref, v_ref, o_ref, s_a, s_b, mx_a, mx_b, acc_sc):
    n_q = q_ref.shape[1] // blk
    buf_a, buf_b = (s_a, mx_a), (s_b, mx_b)
    lane = lax.broadcasted_iota(jnp.int32, (blk, LANES), 1)
    r2 = lax.broadcasted_iota(jnp.int32, (2 * blk, blk), 0)
    c2 = lax.broadcasted_iota(jnp.int32, (2 * blk, blk), 1)
    causal = c2 <= jnp.where(r2 >= blk, r2 - blk, r2)
    lam = _diff_lambda(lq1_ref[...], lk1_ref[...], lq2_ref[...], lk2_ref[...], lam_init)
    gain = sg_ref[...] * (1.0 - lam_init)

    def fold(s):
        out = s[:, :LANES]
        for i in range(1, blk // LANES):
            out = jnp.maximum(out, s[:, i * LANES:(i + 1) * LANES])
        return out

    def stacked_q(qi):
        q = q_ref[0, pl.ds(pl.multiple_of(qi * blk, blk), blk), :]
        zero = jnp.zeros_like(q)
        return jnp.concatenate([jnp.where(lane < DA_HEAD_DIM, q, zero),
                                jnp.where(lane >= DA_HEAD_DIM, q, zero)], axis=0)

    def score_tile(q_st, buf, j):
        s_ref, mx_ref = buf
        s = _nt_dot(q_st, k_ref[0, pl.ds(pl.multiple_of(j * blk, blk), blk), :])
        s_ref[j] = s
        mx_ref[...] = jnp.maximum(mx_ref[...], fold(s))

    def diag_tile(qi, buf):
        s_ref, mx_ref = buf
        q0 = pl.multiple_of(qi * blk, blk)
        s = jnp.where(causal, _nt_dot(stacked_q(qi), k_ref[0, pl.ds(q0, blk), :]), NEG_BIG)
        s_ref[qi] = s
        m = jnp.max(jnp.maximum(mx_ref[...], fold(s)), axis=1, keepdims=True)
        mx_ref[...] = jnp.broadcast_to(m, (2 * blk, LANES))

    def prob_tile(buf, j):
        s_ref, mx_ref = buf
        mb = mx_ref[...]
        p = jnp.exp2(s_ref[j] - jnp.concatenate([mb] * (blk // LANES), axis=1)).astype(BF16)
        acc_sc[...] += jnp.dot(p, v_ref[0, pl.ds(pl.multiple_of(j * blk, blk), blk), :],
                               preferred_element_type=F32)

    def finish(qi):
        acc = acc_sc[...]
        o = (acc[:blk, :DA_V_DIM] / acc[:blk, DA_V_DIM:]
             - lam * (acc[blk:, :DA_V_DIM] / acc[blk:, DA_V_DIM:]))
        o = _rmsnorm_rows(o, gain, SUBLN_EPS)
        o_ref[0, pl.ds(pl.multiple_of(qi * blk, blk), blk), :] = o.astype(o_ref.dtype)

    neg = jnp.full((2 * blk, LANES), NEG_BIG, F32)

    def fused_block(qi, cur, nxt):
        q_next = stacked_q(qi + 1)
        nxt[1][...] = neg
        acc_sc[...] = jnp.zeros(acc_sc.shape, F32)

        def tile(j):
            prob_tile(cur, j)
            score_tile(q_next, nxt, j)

        def two_tiles(j2, c):
            tile(2 * j2)
            tile(2 * j2 + 1)
            return c

        n_t = qi + 1
        lax.fori_loop(0, n_t // 2, two_tiles, 0)

        @pl.when(n_t % 2 == 1)
        def _():
            tile(n_t - 1)

        finish(qi)
        diag_tile(qi + 1, nxt)

    mx_a[...] = neg
    diag_tile(0, buf_a)

    def pair(i2, carry):
        fused_block(2 * i2, buf_a, buf_b)
        fused_block(2 * i2 + 1, buf_b, buf_a)
        return carry

    n_fused = n_q - 1
    lax.fori_loop(0, n_fused // 2, pair, 0)
    if n_fused % 2:
        fused_block(n_fused - 1, buf_a, buf_b)
    last = buf_b if n_fused % 2 else buf_a
    acc_sc[...] = jnp.zeros(acc_sc.shape, F32)

    def tail(j2, c):
        prob_tile(last, 2 * j2)
        prob_tile(last, 2 * j2 + 1)
        return c

    lax.fori_loop(0, n_q // 2, tail, 0)
    if n_q % 2:
        prob_tile(last, n_q - 1)
    finish(n_q - 1)


def _attn_prompt(q, kb, vb1, lams, sg, lam_init, blk):
    b, s, _ = q.shape
    small = pl.BlockSpec((1, DA_HEAD_DIM), lambda bi, h: (0, 0))
    per_head = pl.BlockSpec((1, s, LANES), lambda bi, h: (bi, 0, h))
    return pl.pallas_call(
        functools.partial(_attn_prompt_body, lam_init, blk),
        grid=(b, DA_HEADS),
        in_specs=[small, small, small, small,
                  pl.BlockSpec((1, DA_V_DIM), lambda bi, h: (0, 0)),
                  per_head, per_head,
                  pl.BlockSpec((1, s, 2 * DA_V_DIM), lambda bi, h: (bi, 0, h))],
        out_specs=per_head,
        out_shape=jax.ShapeDtypeStruct((b, s, DA_WIDTH), BF16),
        scratch_shapes=[pltpu.VMEM((s // blk, 2 * blk, blk), F32),
                        pltpu.VMEM((s // blk, 2 * blk, blk), F32),
                        pltpu.VMEM((2 * blk, LANES), F32),
                        pltpu.VMEM((2 * blk, LANES), F32),
                        pltpu.VMEM((2 * blk, 2 * DA_V_DIM), F32)],
        compiler_params=_cparams(("parallel", "parallel")),
        name="attn_prompt",
    )(*lams, sg, q, kb, vb1)


def _attn_sample_body(lam_init, n_tok, pages_per_step, pt_ref, lq1_ref, lk1_ref, lq2_ref, lk2_ref,
                      sg_ref, q_ref, kn_ref, vn_ref, *rest):
    kp = rest[:pages_per_step]
    vp = rest[pages_per_step:2 * pages_per_step]
    o_ref = rest[2 * pages_per_step]
    m_sc, l_sc, a_sc = rest[2 * pages_per_step + 1:]
    j = pl.program_id(1)
    nrow = DA_HEADS * 2 * n_tok
    ncol = PAGE_SIZE * DA_HEADS

    @pl.when(j == 0)
    def _():
        m_sc[...] = jnp.full(m_sc.shape, NEG_BIG, F32)
        l_sc[...] = jnp.zeros(l_sc.shape, F32)
        a_sc[...] = jnp.zeros(a_sc.shape, F32)

    q = q_ref[0]
    row = lax.broadcasted_iota(jnp.int32, (nrow, ncol), 0)
    colk = lax.broadcasted_iota(jnp.int32, (nrow, ncol), 1)
    same_head = (colk % DA_HEADS) == (row // (2 * n_tok))

    def update(s_list, v_list):
        m = m_sc[...]
        mn = m
        for s in s_list:
            mn = jnp.maximum(mn, jnp.max(s, axis=-1, keepdims=True))
        alpha = jnp.exp2(m - mn)
        l = alpha * l_sc[...]
        a = alpha * a_sc[...]
        for s, v in zip(s_list, v_list):
            p = jnp.exp2(s - mn)
            l = l + jnp.sum(p, axis=-1, keepdims=True)
            a = a + jnp.dot(p.astype(BF16), v, preferred_element_type=F32)
        m_sc[...] = mn
        l_sc[...] = l
        a_sc[...] = a

    s_list, v_list = [], []
    for i in range(pages_per_step):
        kb = kp[i][0, 0].astype(BF16)
        s_list.append(jnp.where(same_head, _nt_dot(q, kb), NEG_BIG))
        v_list.append(vp[i][0, 0].astype(BF16))
    update(s_list, v_list)

    @pl.when(j == pl.num_programs(1) - 1)
    def _():
        rown = lax.broadcasted_iota(jnp.int32, (nrow, kn_ref.shape[1]), 0)
        coln = lax.broadcasted_iota(jnp.int32, (nrow, kn_ref.shape[1]), 1)
        ok = ((coln < n_tok * DA_HEADS) & ((coln % DA_HEADS) == (rown // (2 * n_tok)))
              & ((coln // DA_HEADS) <= (rown % n_tok)))
        s = jnp.where(ok, _nt_dot(q, kn_ref[0]), NEG_BIG)
        update([s], [vn_ref[0]])
        a = a_sc[...] / l_sc[...]
        lam = _diff_lambda(lq1_ref[...], lk1_ref[...], lq2_ref[...], lk2_ref[...], lam_init)
        for h in range(DA_HEADS):
            base = h * 2 * n_tok
            o = a[base:base + n_tok] - lam * a[base + n_tok:base + 2 * n_tok]
            o = _rmsnorm_rows(o, sg_ref[...], SUBLN_EPS) * (1.0 - lam_init)
            o_ref[0, :, h * LANES:(h + 1) * LANES] = o


def _attn_sample(layer, q_rows, kn, vn, cache_k, cache_v, page_table, lams, sg, lam_init,
                 n_tok, pages_per_step):
    db, nrow, _ = q_rows.shape
    n_pages = page_table.shape[1]
    assert n_pages % pages_per_step == 0
    pt = page_table.reshape(-1)
    small = pl.BlockSpec((1, DA_HEAD_DIM), lambda b, j, p: (0, 0))
    per_seq = lambda b, j, p: (b, 0, 0)

    def page_spec(i):
        return pl.BlockSpec(
            (1, 1, PAGE_SIZE * DA_HEADS, LANES),
            lambda b, j, p: (layer, p[b * n_pages + j * pages_per_step + i], 0, 0))

    pages = [page_spec(i) for i in range(pages_per_step)]
    grid_spec = pltpu.PrefetchScalarGridSpec(
        num_scalar_prefetch=1,
        grid=(db, n_pages // pages_per_step),
        in_specs=[small, small, small, small,
                  pl.BlockSpec((1, DA_V_DIM), lambda b, j, p: (0, 0)),
                  pl.BlockSpec((1, nrow, LANES), per_seq),
                  pl.BlockSpec((1,) + kn.shape[1:], per_seq),
                  pl.BlockSpec((1,) + vn.shape[1:], per_seq)] + pages + pages,
        out_specs=pl.BlockSpec((1, n_tok, DA_WIDTH), per_seq),
        scratch_shapes=[pltpu.VMEM((nrow, 1), F32), pltpu.VMEM((nrow, 1), F32),
                        pltpu.VMEM((nrow, DA_V_DIM), F32)],
    )
    return pl.pallas_call(
        functools.partial(_attn_sample_body, lam_init, n_tok, pages_per_step),
        grid_spec=grid_spec,
        out_shape=jax.ShapeDtypeStruct((db, n_tok, DA_WIDTH), F32),
        compiler_params=_cparams(("parallel", "arbitrary")),
        name="attn_sample",
    )(pt, *lams, sg, q_rows, kn, vn, *([cache_k] * pages_per_step), *([cache_v] * pages_per_step))


def _mix_out_prompt_body(n_chunk, a_ref, u_ref, gv_ref, h_ref, ws_ref, bst_ref, wo_ref, gf_ref,
                         ho_ref, xn_ref, g_sc):
    r = lax.broadcasted_iota(jnp.int32, (CHUNK, CHUNK), 0)
    c = lax.broadcasted_iota(jnp.int32, (CHUNK, CHUNK), 1)
    for g in range(GM_GROUPS):
        w = jnp.where(c <= r, ws_ref[g], 0.0).astype(BF16)
        bias = bst_ref[:, g:g + 1]
        lanes = slice(g * GM_CH, (g + 1) * GM_CH)
        for ci in range(n_chunk):
            rows = slice(ci * CHUNK, (ci + 1) * CHUNK)
            m = jnp.dot(w, gv_ref[rows, lanes], preferred_element_type=F32) + bias
            g_sc[rows, lanes] = (u_ref[rows, lanes].astype(F32) * m).astype(BF16)
    out = (jnp.dot(a_ref[...], wo_ref[:DA_WIDTH, :], preferred_element_type=F32)
           + jnp.dot(g_sc[...], wo_ref[DA_WIDTH:, :], preferred_element_type=F32))
    hn = h_ref[...] + out
    ho_ref[...] = hn
    _put_rows(xn_ref, _rmsnorm_rows(hn, gf_ref[...], RMS_EPS).astype(xn_ref.dtype))


def _mix_out_sample_body(n_tok, n_seq, a_ref, u_ref, gv_ref, h_ref, ws_ref, bs_ref, wo_ref, gf_ref,
                         ho_ref, xn_ref, g_sc):
    for g in range(GM_GROUPS):
        lanes = slice(g * GM_CH, (g + 1) * GM_CH)
        for t in range(n_tok):
            m = jnp.zeros((n_seq, GM_CH), F32) + bs_ref[g:g + 1, t:t + 1]
            for s in range(t + 1):
                m = m + ws_ref[g, t:t + 1, s:s + 1] * gv_ref[s * n_seq:(s + 1) * n_seq, lanes]
            rows = slice(t * n_seq, (t + 1) * n_seq)
            g_sc[rows, lanes] = u_ref[rows, lanes] * m
    out = (jnp.dot(a_ref[...], wo_ref[:DA_WIDTH, :], preferred_element_type=F32)
           + jnp.dot(g_sc[...].astype(BF16), wo_ref[DA_WIDTH:, :], preferred_element_type=F32))
    hn = h_ref[...] + out
    ho_ref[...] = hn
    _put_rows(xn_ref, _rmsnorm_rows(hn, gf_ref[...], RMS_EPS).astype(xn_ref.dtype))


def _mix_out(a, u, gv, h, ws, bs, wo_bf, gf, tm, sample_shape=None, tile_rows=False):
    m, d = h.shape
    row = lambda i: (i, 0)
    tpr = _tiles_per_row(d) if tile_rows else 1
    fixed2 = lambda i: (0, 0)
    fixed3 = lambda i: (0, 0, 0)
    if sample_shape is None:
        body = functools.partial(_mix_out_prompt_body, tm // CHUNK)
        bias = bs.T
        g_dtype = BF16
    else:
        body = functools.partial(_mix_out_sample_body, *sample_shape)
        bias = bs
        g_dtype = F32
    return pl.pallas_call(
        body,
        grid=(m // tm,),
        in_specs=[pl.BlockSpec((tm, DA_WIDTH), row),
                  pl.BlockSpec((tm, GM_WIDTH), row),
                  pl.BlockSpec((tm, GM_WIDTH), row),
                  pl.BlockSpec((tm, d), row),
                  pl.BlockSpec(ws.shape, fixed3),
                  pl.BlockSpec(bias.shape, fixed2),
                  pl.BlockSpec(wo_bf.shape, fixed2),
                  pl.BlockSpec((1, d), fixed2)],
        out_specs=[pl.BlockSpec((tm, d), row), pl.BlockSpec((tm * tpr, d // tpr), row)],
        out_shape=[jax.ShapeDtypeStruct((m, d), F32),
                   jax.ShapeDtypeStruct((m * tpr, d // tpr), F32 if tile_rows else BF16)],
        scratch_shapes=[pltpu.VMEM((tm, GM_WIDTH), g_dtype)],
        compiler_params=_cparams(("parallel",)),
        name="mix_out",
    )(a, u, gv, h, ws, bias, wo_bf, gf)


def _swiglu(x, wg_ref, wu_ref, wd_ref):
    g = jnp.dot(x, wg_ref[...], preferred_element_type=F32)
    u = jnp.dot(x, wu_ref[...], preferred_element_type=F32)
    hid = (g * jax.nn.sigmoid(g) * u).astype(BF16)
    return jnp.dot(hid, wd_ref[...], preferred_element_type=F32)


def _ffn_body(xn_ref, h_ref, wg_ref, wu_ref, wd_ref, o_ref):
    o_ref[...] = h_ref[...] + _swiglu(xn_ref[...], wg_ref, wu_ref, wd_ref)


def _ffn(xn, h, wg, wu, wd, tm):
    m, d = h.shape
    ff = wg.shape[1]
    resident = dict(pipeline_mode=pl.Buffered(1))
    return pl.pallas_call(
        _ffn_body,
        grid=(m // tm,),
        in_specs=[pl.BlockSpec((tm, d), lambda i: (i, 0)),
                  pl.BlockSpec((tm, d), lambda i: (i, 0)),
                  pl.BlockSpec((d, ff), lambda i: (0, 0), **resident),
                  pl.BlockSpec((d, ff), lambda i: (0, 0), **resident),
                  pl.BlockSpec((ff, d), lambda i: (0, 0), **resident)],
        out_specs=pl.BlockSpec((tm, d), lambda i: (i, 0)),
        out_shape=jax.ShapeDtypeStruct((m, d), F32),
        compiler_params=_cparams(("parallel",)),
        name="ffn",
    )(xn, h, wg, wu, wd)


R_E1, R_E2, R_G1, R_G2, R_RANK1, R_RANK2 = range(6)


def _router_body(n_exp, xn_ref, wr_ref, base_ref, o_ref, cnt_ref, cnt_sc):
    i = pl.program_id(0)

    @pl.when(i == 0)
    def _():
        cnt_sc[...] = base_ref[...]

    d = wr_ref.shape[0]
    tm = xn_ref.shape[0] // _tiles_per_row(d)
    logits = jnp.dot(_get_rows(xn_ref, tm, d), wr_ref[...], preferred_element_type=F32,
                     precision=lax.Precision.HIGHEST)
    col = lax.broadcasted_iota(jnp.int32, logits.shape, 1)
    logits = jnp.where(col < n_exp, logits, -jnp.inf)
    big = jnp.int32(LANES)
    v1 = jnp.max(logits, axis=-1, keepdims=True)
    i1 = jnp.min(jnp.where(logits == v1, col, big), axis=-1, keepdims=True)
    rest = jnp.where(col == i1, -jnp.inf, logits)
    v2 = jnp.max(rest, axis=-1, keepdims=True)
    i2 = jnp.min(jnp.where(rest == v2, col, big), axis=-1, keepdims=True)
    e = jnp.exp(v2 - v1)
    g1 = 1.0 / (1.0 + e)
    g2 = e / (1.0 + e)
    hit = (col == i1) | (col == i2)
    r = lax.broadcasted_iota(jnp.int32, (tm, tm), 0)
    c = lax.broadcasted_iota(jnp.int32, (tm, tm), 1)
    below = jnp.where(c < r, 1.0, 0.0).astype(BF16)
    prefix = jnp.dot(below, jnp.where(hit, 1.0, 0.0).astype(BF16), preferred_element_type=F32)
    rank = prefix + cnt_sc[...]
    r1 = jnp.sum(jnp.where(col == i1, rank, 0.0), axis=-1, keepdims=True)
    r2 = jnp.sum(jnp.where(col == i2, rank, 0.0), axis=-1, keepdims=True)
    cnt_sc[...] += jnp.sum(jnp.where(hit, 1.0, 0.0), axis=0, keepdims=True)
    rec = jnp.zeros(logits.shape, F32)
    for idx, val in ((R_E1, i1.astype(F32)), (R_E2, i2.astype(F32)), (R_G1, g1), (R_G2, g2),
                     (R_RANK1, r1), (R_RANK2, r2)):
        rec = jnp.where(col == idx, val, rec)
    o_ref[...] = rec
    cnt_ref[...] = cnt_sc[...]


def _router(xn_t, wr_pad, base_counts, n_exp, tm):
    tpr = _tiles_per_row(wr_pad.shape[0])
    m = xn_t.shape[0] // tpr
    return pl.pallas_call(
        functools.partial(_router_body, n_exp),
        grid=(m // tm,),
        in_specs=[pl.BlockSpec((tm * tpr, LANES), lambda i: (i, 0)),
                  pl.BlockSpec(wr_pad.shape, lambda i: (0, 0)),
                  pl.BlockSpec((1, LANES), lambda i: (0, 0))],
        out_specs=[pl.BlockSpec((tm, LANES), lambda i: (i, 0)),
                   pl.BlockSpec((1, LANES), lambda i: (0, 0))],
        out_shape=[jax.ShapeDtypeStruct((m, LANES), F32), jax.ShapeDtypeStruct((1, LANES), F32)],
        scratch_shapes=[pltpu.VMEM((1, LANES), F32)],
        compiler_params=_cparams(("arbitrary",)),
        name="router",
    )(xn_t, wr_pad, base_counts)


def _tile_copy(src, src_row, dst, dst_row, tpr, sem):
    return pltpu.make_async_copy(src.at[pl.ds(pl.multiple_of(src_row * tpr, tpr), tpr), :],
                                 dst.at[pl.ds(pl.multiple_of(dst_row * tpr, tpr), tpr), :], sem)


def _dispatch_body(tm, tpr, slot_ref, xn_ref, xs_in_ref, xs_ref, sem):
    del xs_in_ref

    def start(t, c):
        for k in range(TOP_K):
            _tile_copy(xn_ref, t, xs_ref, slot_ref[0, 0, TOP_K * t + k], tpr, sem).start(
                priority=k % 2)
        return c

    def wait(t, c):
        for k in range(TOP_K):
            _tile_copy(xn_ref, t, xs_ref, t, tpr, sem).wait()
        return c

    lax.fori_loop(0, tm, start, 0, unroll=DMA_UNROLL)
    lax.fori_loop(0, tm, wait, 0, unroll=DMA_UNROLL)


def _dispatch(slots, xn_t, xs_t, tm, tpr):
    m = xn_t.shape[0] // tpr
    slots3 = slots.reshape(m // tm, 1, tm * TOP_K)
    return pl.pallas_call(
        functools.partial(_dispatch_body, tm, tpr),
        grid=(m // tm,),
        in_specs=[pl.BlockSpec((1, 1, tm * TOP_K), lambda i: (i, 0, 0), memory_space=pltpu.SMEM),
                  pl.BlockSpec((tm * tpr, LANES), lambda i: (i, 0)),
                  pl.BlockSpec(memory_space=pl.ANY)],
        out_specs=pl.BlockSpec(memory_space=pl.ANY),
        out_shape=jax.ShapeDtypeStruct(xs_t.shape, xs_t.dtype),
        scratch_shapes=[pltpu.SemaphoreType.DMA(())],
        input_output_aliases={2: 0},
        compiler_params=_cparams(("arbitrary",)),
        name="moe_dispatch",
    )(slots3, xn_t, xs_t)


def _combine_body(tm, tpr, slot_cur_ref, slot_nxt_ref, rec_ref, h_ref, g_ref, ys_ref, o_ref,
                  buf, sem):
    i = pl.program_id(0)
    n = pl.num_programs(0)
    d = h_ref.shape[1]

    def fetch(slot_ref, half):
        def start(t, c):
            for k in range(TOP_K):
                _tile_copy(ys_ref, slot_ref[0, 0, TOP_K * t + k], buf.at[half, k], t, tpr,
                           sem.at[half]).start(priority=k % 2)
            return c
        lax.fori_loop(0, tm, start, 0, unroll=DMA_UNROLL)

    def wait_block(half):
        def wait(t, c):
            for k in range(TOP_K):
                _tile_copy(ys_ref, t, buf.at[half, k], t, tpr, sem.at[half]).wait()
            return c
        lax.fori_loop(0, tm, wait, 0, unroll=DMA_UNROLL)

    def step(half):
        @pl.when(i == 0)
        def _():
            fetch(slot_cur_ref, half)

        @pl.when(i + 1 < n)
        def _():
            fetch(slot_nxt_ref, 1 - half)

        wait_block(half)
        rec = rec_ref[...]
        y = (rec[:, R_G1:R_G1 + 1] * _get_rows(buf.at[half, 0], tm, d)
             + rec[:, R_G2:R_G2 + 1] * _get_rows(buf.at[half, 1], tm, d))
        o_ref[...] = _rmsnorm_rows(h_ref[...] + y, g_ref[...], RMS_EPS)

    @pl.when(i % 2 == 0)
    def _():
        step(0)

    @pl.when(i % 2 == 1)
    def _():
        step(1)


def _combine_norm(slots, rec, h, g, ys_t, tm):
    m, d = h.shape
    tpr = _tiles_per_row(d)
    n = m // tm
    slots3 = slots.reshape(n, 1, tm * TOP_K)
    row = lambda i: (i, 0)
    return pl.pallas_call(
        functools.partial(_combine_body, tm, tpr),
        grid=(n,),
        in_specs=[pl.BlockSpec((1, 1, tm * TOP_K), lambda i: (i, 0, 0), memory_space=pltpu.SMEM),
                  pl.BlockSpec((1, 1, tm * TOP_K), lambda i: (jnp.minimum(i + 1, n - 1), 0, 0),
                               memory_space=pltpu.SMEM),
                  pl.BlockSpec((tm, LANES), row),
                  pl.BlockSpec((tm, d), row),
                  pl.BlockSpec((1, d), lambda i: (0, 0)),
                  pl.BlockSpec(memory_space=pl.ANY)],
        out_specs=pl.BlockSpec((tm, d), row),
        out_shape=jax.ShapeDtypeStruct((m, d), F32),
        scratch_shapes=[pltpu.VMEM((2, TOP_K, tm * tpr, LANES), F32),
                        pltpu.SemaphoreType.DMA((2,))],
        compiler_params=_cparams(("arbitrary",)),
        name="moe_combine_norm",
    )(slots3, slots3, rec, h, g, ys_t)


def _moe_body(tm, be_ref, nv_ref, xs_ref, wg_ref, wu_ref, wd_ref, o_ref):
    live = pl.program_id(0) < nv_ref[0]
    d = wg_ref.shape[1]

    @pl.when(live)
    def _():
        x = _get_rows(xs_ref, tm, d).astype(BF16)
        _put_rows(o_ref, _swiglu(x, wg_ref.at[0], wu_ref.at[0], wd_ref.at[0]))

    @pl.when(jnp.logical_not(live))
    def _():
        o_ref[...] = jnp.zeros(o_ref.shape, F32)


def _moe_experts(xs_t, block_expert, n_valid, wg, wu, wd, tm):
    d, ff = wg.shape[1], wg.shape[2]
    tpr = _tiles_per_row(d)
    p = xs_t.shape[0] // tpr
    rows = pl.BlockSpec((tm * tpr, LANES), lambda i, be, nv: (i, 0))
    resident = dict(pipeline_mode=pl.Buffered(1))
    grid_spec = pltpu.PrefetchScalarGridSpec(
        num_scalar_prefetch=2,
        grid=(p // tm,),
        in_specs=[rows,
                  pl.BlockSpec((1, d, ff), lambda i, be, nv: (be[i], 0, 0), **resident),
                  pl.BlockSpec((1, d, ff), lambda i, be, nv: (be[i], 0, 0), **resident),
                  pl.BlockSpec((1, ff, d), lambda i, be, nv: (be[i], 0, 0), **resident)],
        out_specs=rows,
    )
    return pl.pallas_call(
        functools.partial(_moe_body, tm),
        grid_spec=grid_spec,
        out_shape=jax.ShapeDtypeStruct(xs_t.shape, F32),
        compiler_params=_cparams(("arbitrary",)),
        name="moe_experts",
    )(block_expert, n_valid, xs_t, wg, wu, wd)


def _moe_final(groups, w_router, wg, wu, wd, final_g, tm):
    d = w_router.shape[0]
    n_exp = w_router.shape[1]
    n_tok = sum(h.shape[0] for _, h, _ in groups)
    wr_pad = jnp.zeros((d, LANES), F32).at[:, :n_exp].set(w_router)
    counts = jnp.zeros((1, LANES), F32)
    recs = []
    for xn, h, _ in groups:
        rec, counts = _router(xn, wr_pad, counts, n_exp, _pick(h.shape[0], 512))
        recs.append(rec)
    cnt = counts[0, :n_exp].astype(jnp.int32)
    padded = ((cnt + tm - 1) // tm) * tm
    group_end = jnp.cumsum(padded)
    group_start = group_end - padded
    n_slots = ((n_tok * TOP_K + n_exp * (tm - 1)) // tm) * tm
    n_blocks = n_slots // tm
    block_start = jnp.arange(n_blocks, dtype=jnp.int32) * tm
    n_valid = (group_end[-1] // tm).astype(jnp.int32)
    block_expert = jnp.sum((block_start[:, None] >= group_end[None, :]).astype(jnp.int32), axis=1)
    last_valid = jnp.sum((jnp.maximum(n_valid - 1, 0) * tm >= group_end).astype(jnp.int32))
    block_expert = jnp.minimum(jnp.where(block_start < n_valid * tm, block_expert, last_valid),
                               n_exp - 1).astype(jnp.int32)

    def slots_of(rec):
        e = rec[:, R_E1:R_E2 + 1].astype(jnp.int32)
        rank = rec[:, R_RANK1:R_RANK2 + 1].astype(jnp.int32)
        start = jnp.sum(jnp.where(e[..., None] == jnp.arange(n_exp), group_start, 0), axis=-1)
        return (start + rank).astype(jnp.int32)

    slots = [slots_of(rec) for rec in recs]
    tpr = _tiles_per_row(d)
    xs = jnp.zeros((n_slots * tpr, LANES), F32)
    for (xn, _, tb), sl in zip(groups, slots):
        xs = _dispatch(sl, xn, xs, _pick(sl.shape[0], 4 * tb), tpr)
    ys = _moe_experts(xs, block_expert, n_valid.reshape(1), wg, wu, wd, tm)
    return [_combine_norm(sl, rec, h, final_g, ys, tb)
            for (_, h, tb), sl, rec in zip(groups, slots, recs)]


def kernel(x_prompt, x_sample, cache_k, cache_v, page_table, w_in, w_out, norm_mix_g, norm_ffn_g,
           lam_q1, lam_k1, lam_q2, lam_k2, subln_g, gm_ln_g, gm_ln_b, gm_ws, gm_bs,
           ffn_wg, ffn_wu, ffn_wd, moe_router, moe_wg, moe_wu, moe_wd, final_norm_g):
    b, s, d = x_prompt.shape
    db, t = x_sample.shape[:2]
    depth = w_in.shape[0]
    n_pages = page_table.shape[1]
    past = n_pages * PAGE_SIZE
    n_p, n_s = b * s, db * t
    assert depth == 2 and s % CHUNK == 0

    plan = _tile_plan(s, n_p, n_pages)
    tm_p, blk, pages_per_step = plan["rows"], plan["attn_blk"], plan["pages_per_step"]
    to_bf16 = _to_bf16
    w_in_bf, w_out_bf = to_bf16(w_in), to_bf16(w_out)

    hp = x_prompt.reshape(n_p, d)
    hs = x_sample.transpose(1, 0, 2).reshape(n_s, d)
    tab_p = _rope_tables(jnp.arange(s, dtype=F32))
    pos_s = jnp.arange(t, dtype=F32) + jnp.float32(past)
    tab_s = _rope_tables(jnp.repeat(pos_s, db))
    ck = cache_k.reshape(cache_k.shape[0], cache_k.shape[1], PAGE_SIZE * DA_HEADS, LANES)
    cv = cache_v.reshape(cache_v.shape[0], cache_v.shape[1], PAGE_SIZE * DA_HEADS, LANES)
    lane = jnp.arange(LANES)
    comp_mask = jnp.stack([lane < DA_HEAD_DIM, lane >= DA_HEAD_DIM])
    new_rows = PAGE_SIZE
    assert t * DA_HEADS <= new_rows

    outs = {k: [] for k in ("ks", "vs", "gv")}
    y_p = y_s = k_all = v_all = None
    for l in range(depth):
        lam_init = 0.8 - 0.6 * math.exp(-0.3 * l)
        g_mix = norm_mix_g[l][None]
        g_ffn = norm_ffn_g[l][None]
        lng, lnb = gm_ln_g[l][None], gm_ln_b[l][None]
        lams = (lam_q1[l][None], lam_k1[l][None], lam_q2[l][None], lam_k2[l][None])
        sg = subln_g[l][None]

        q, k_all, v_all, kb, vb, u, gv = _proj(hp, g_mix, w_in_bf[l], tab_p, lng, lnb, tm_p, BF16,
                                               BF16, kv_stack=(l, depth, k_all, v_all))
        a = _attn_prompt(q.reshape(b, s, QK_COLS), kb.reshape(b, s, QK_COLS),
                         vb.reshape(b, s, 2 * DA_WIDTH), lams, sg, lam_init, blk)
        moe_layer = l % 2 == 1
        hp, xn_p = _mix_out(a.reshape(n_p, DA_WIDTH), u, gv, hp, gm_ws[l], gm_bs[l], w_out_bf[l],
                            g_ffn, tm_p, tile_rows=moe_layer)

        sq, sk, sv, skb, svb, su, sgv = _proj(hs, g_mix, w_in_bf[l], tab_s, lng, lnb, n_s, F32, F32)
        q5 = sq.reshape(t, db, DA_HEADS, 1, LANES).transpose(1, 2, 3, 0, 4)
        q_rows = jnp.where(comp_mask[None, None, :, None, :], q5, jnp.zeros((), BF16))
        q_rows = q_rows.reshape(db, DA_HEADS * 2 * t, LANES)
        pad = ((0, 0), (0, new_rows - t * DA_HEADS), (0, 0))
        kn = jnp.pad(skb.reshape(t, db, DA_HEADS, LANES).transpose(1, 0, 2, 3).reshape(db, t * DA_HEADS, LANES), pad)
        svb = svb.reshape(t, db, DA_HEADS, 2 * DA_V_DIM)[..., :DA_V_DIM]
        vn = jnp.pad(svb.transpose(1, 0, 2, 3).reshape(db, t * DA_HEADS, LANES), pad)
        sa = _attn_sample(l, q_rows, kn, vn, ck, cv, page_table, lams, sg, lam_init, t, pages_per_step)
        sa = sa.transpose(1, 0, 2).reshape(n_s, DA_WIDTH).astype(BF16)
        hs, xn_s = _mix_out(sa, su, sgv, hs, gm_ws[l], gm_bs[l], w_out_bf[l], g_ffn, n_s,
                            sample_shape=(t, db), tile_rows=moe_layer)

        outs["ks"].append(sk.reshape(t, db, DA_HEADS, 2 * DA_HEAD_DIM).transpose(1, 0, 2, 3))
        outs["vs"].append(sv.reshape(t, db, DA_HEADS, DA_V_DIM).transpose(1, 0, 2, 3))
        outs["gv"].append(sgv.reshape(t, db, GM_GROUPS, GM_CH).transpose(1, 0, 2, 3))

        j = l // 2
        if l % 2 == 0:
            wg, wu, wd = to_bf16(ffn_wg[j]), to_bf16(ffn_wu[j]), to_bf16(ffn_wd[j])
            hp = _ffn(xn_p, hp, wg, wu, wd, tm_p)
            hs = _ffn(xn_s, hs, wg, wu, wd, n_s)
        else:
            wg, wu, wd = to_bf16(moe_wg[j]), to_bf16(moe_wu[j]), to_bf16(moe_wd[j])
            y_p, y_s = _moe_final([(xn_p, hp, plan["combine_rows"]), (xn_s, hs, n_s)],
                                  moe_router[j], wg, wu, wd, final_norm_g[None],
                                  plan["expert_rows"])

    y_prompt = y_p.reshape(b, s, d)
    y_sample = y_s.reshape(t, db, d).transpose(1, 0, 2)
    return (y_prompt, y_sample,
            k_all.reshape(depth, b, s, DA_HEADS, 2 * DA_HEAD_DIM),
            v_all.reshape(depth, b, s, DA_HEADS, DA_V_DIM),
            jnp.stack(outs["ks"]), jnp.stack(outs["vs"]), jnp.stack(outs["gv"]))
```

```python
import functools
import math

import jax
import jax.numpy as jnp
from jax import lax
from jax.experimental import pallas as pl
from jax.experimental.pallas import tpu as pltpu

F32 = jnp.float32
BF16 = jnp.bfloat16

DA_HEADS = 4
DA_HEAD_DIM = 64
DA_V_DIM = 2 * DA_HEAD_DIM
QK_COLS = DA_HEADS * 2 * DA_HEAD_DIM
DA_WIDTH = DA_HEADS * DA_V_DIM
ROT_DIM = DA_HEAD_DIM // 4
ROPE_THETA = 500000.0
GM_GROUPS = 4
GM_CH = 128
GM_WIDTH = GM_GROUPS * GM_CH
CHUNK = 128
PAGE_SIZE = 128
TOP_K = 2
RMS_EPS = 1e-6
SUBLN_EPS = 1e-5
LN_EPS = 1e-5
NEG_BIG = -1e30
LOG2_E = math.log2(math.e)
LANES = 128
V7X_VMEM_BYTES = 64 * 1024 * 1024
VMEM_LIMIT = V7X_VMEM_BYTES * 13 // 16
DMA_UNROLL = 8


def _pick(total, pref):
    t = min(total, pref)
    while total % t:
        t //= 2
    return t


def _tile_plan(s, n_p, n_pages):
    rows = _pick(s, 512)
    assert rows % CHUNK == 0
    return dict(
        rows=rows,
        attn_blk=_pick(s, 512),
        pages_per_step=_pick(n_pages, 32),
        expert_rows=512,
        combine_rows=_pick(n_p, 512),
    )


def _cparams(sem):
    return pltpu.CompilerParams(dimension_semantics=sem, vmem_limit_bytes=VMEM_LIMIT)


def _to_bf16(w):
    return w.astype(BF16)


def _nt_dot(a, b):
    return lax.dot_general(a, b, (((1,), (1,)), ((), ())), preferred_element_type=F32)


def _rmsnorm_rows(x, g, eps):
    return x * lax.rsqrt(jnp.mean(x * x, axis=-1, keepdims=True) + eps) * g


def _tiles_per_row(d):
    return d // LANES


def _put_rows(ref, x):
    n, d = x.shape
    if ref.shape == x.shape:
        ref[...] = x
        return
    tpr = _tiles_per_row(d)
    assert ref.shape == (n * tpr, LANES)
    for c in range(tpr):
        ref[pl.ds(c, n, stride=tpr), :] = x[:, c * LANES:(c + 1) * LANES]


def _get_rows(ref, n, d):
    tpr = _tiles_per_row(d)
    return jnp.concatenate([ref[pl.ds(c, n, stride=tpr), :] for c in range(tpr)], axis=1)


def _gelu_exact(x):
    return 0.5 * x * (1.0 + lax.erf(x * (2.0 ** -0.5)))


def _diff_lambda(lq1, lk1, lq2, lk2, lam_init):
    a = jnp.exp(jnp.sum(lq1 * lk1, axis=-1, keepdims=True))
    b = jnp.exp(jnp.sum(lq2 * lk2, axis=-1, keepdims=True))
    return a - b + lam_init


def _put_head(ref, t, x):
    if ref.shape[-1] == DA_HEADS * LANES:
        ref[:, t * LANES:(t + 1) * LANES] = x
    else:
        for j in range(ref.shape[0]):
            ref.at[j][pl.ds(t, x.shape[0], stride=DA_HEADS), :] = x


def _proj_body(n_alias, h_ref, g_ref, w_ref, cos_ref, sa_ref, sb_ref, lng_ref, lnb_ref, *rest):
    q_ref, k_ref, v_ref, kb_ref, vb_ref, u_ref, gv_ref = rest[n_alias:]
    h = h_ref[...]
    xn = _rmsnorm_rows(h, g_ref[...], RMS_EPS).astype(BF16)
    z = jnp.dot(xn, w_ref[...], preferred_element_type=F32)
    cos, sa, sb = cos_ref[...], sa_ref[...], sb_ref[...]

    def cols(lo, width):
        return z[:, lo:lo + width]

    def rope(x):
        return (x * cos + pltpu.roll(x, ROT_DIM // 2, 1) * sa
                + pltpu.roll(x, LANES - ROT_DIM // 2, 1) * sb)

    scale = DA_HEAD_DIM ** -0.5 * LOG2_E
    ones = jnp.ones((h.shape[0], DA_V_DIM), vb_ref.dtype)
    zq = cols(0, QK_COLS)
    for t in range(DA_HEADS):
        sl = slice(t * LANES, (t + 1) * LANES)
        q_ref[:, sl] = (rope(zq[:, sl]) * scale).astype(q_ref.dtype)
    zk = cols(QK_COLS, QK_COLS)
    for t in range(DA_HEADS):
        sl = slice(t * LANES, (t + 1) * LANES)
        kt = rope(zk[:, sl])
        _put_head(k_ref, t, kt)
        kb_ref[:, sl] = kt.astype(kb_ref.dtype)
    zv = cols(2 * QK_COLS, DA_WIDTH)
    for t in range(DA_HEADS):
        vt = zv[:, t * DA_V_DIM:(t + 1) * DA_V_DIM]
        _put_head(v_ref, t, vt)
        vb_ref[:, 2 * t * DA_V_DIM:(2 * t + 1) * DA_V_DIM] = vt.astype(vb_ref.dtype)
        vb_ref[:, (2 * t + 1) * DA_V_DIM:(2 * t + 2) * DA_V_DIM] = ones
    u_ref[...] = _gelu_exact(cols(2 * QK_COLS + DA_WIDTH, GM_WIDTH)).astype(u_ref.dtype)
    gvr = _gelu_exact(cols(2 * QK_COLS + DA_WIDTH + GM_WIDTH, GM_WIDTH))
    mu = jnp.mean(gvr, axis=-1, keepdims=True)
    d = gvr - mu
    var = jnp.mean(d * d, axis=-1, keepdims=True)
    gv = d * lax.rsqrt(var + LN_EPS) * lng_ref[...] + lnb_ref[...]
    gv_ref[...] = gv.astype(gv_ref.dtype)


def _proj(h, g, w_bf, tables, lng, lnb, tm, u_dtype, gv_dtype, kv_stack=None):
    m, d = h.shape
    cos_t, sa_t, sb_t = tables
    ntab = cos_t.shape[0] // tm
    row = lambda i: (i, 0)
    fixed = lambda i: (0, 0)
    tab = lambda i: (i % ntab, 0)
    wide = lambda dt: jax.ShapeDtypeStruct((m, QK_COLS), dt)
    in_specs = [
        pl.BlockSpec((tm, d), row),
        pl.BlockSpec((1, d), fixed),
        pl.BlockSpec(w_bf.shape, fixed),
        pl.BlockSpec((tm, LANES), tab),
        pl.BlockSpec((tm, LANES), tab),
        pl.BlockSpec((tm, LANES), tab),
        pl.BlockSpec((1, GM_WIDTH), fixed),
        pl.BlockSpec((1, GM_WIDTH), fixed),
    ]
    args = [h, g, w_bf, cos_t, sa_t, sb_t, lng, lnb]
    kv_spec = pl.BlockSpec((tm, QK_COLS), row)
    kv_shape = wide(F32)
    aliases = {}
    if kv_stack is not None:
        layer, depth, k_all, v_all = kv_stack
        kv_shape = jax.ShapeDtypeStruct((depth, m * DA_HEADS, LANES), F32)
        if k_all is None:
            assert layer == 0
            kv_spec = pl.BlockSpec((depth, tm * DA_HEADS, LANES), lambda i: (0, i, 0))
        else:
            kv_spec = pl.BlockSpec((1, tm * DA_HEADS, LANES), lambda i: (layer, i, 0))
            aliases = {len(args): 1, len(args) + 1: 2}
            in_specs += [pl.BlockSpec(memory_space=pl.ANY)] * 2
            args += [k_all, v_all]
    return pl.pallas_call(
        functools.partial(_proj_body, len(aliases)),
        grid=(m // tm,),
        in_specs=in_specs,
        out_specs=[pl.BlockSpec((tm, QK_COLS), row), kv_spec, kv_spec,
                   pl.BlockSpec((tm, QK_COLS), row), pl.BlockSpec((tm, 2 * DA_WIDTH), row),
                   pl.BlockSpec((tm, GM_WIDTH), row), pl.BlockSpec((tm, GM_WIDTH), row)],
        out_shape=[wide(BF16), kv_shape, kv_shape, wide(BF16),
                   jax.ShapeDtypeStruct((m, 2 * DA_WIDTH), BF16), wide(u_dtype), wide(gv_dtype)],
        input_output_aliases=aliases,
        compiler_params=_cparams(("parallel",)),
        name="proj",
    )(*args)


def _rope_tables(pos):
    inv = ROPE_THETA ** (-jnp.arange(0, ROT_DIM, 2, dtype=F32) / ROT_DIM)
    ang = pos[:, None] * inv[None, :]
    c, s = jnp.cos(ang), jnp.sin(ang)
    j = jnp.arange(LANES) % DA_HEAD_DIM
    first = j < ROT_DIM // 2
    second = (j >= ROT_DIM // 2) & (j < ROT_DIM)
    idx = j % (ROT_DIM // 2)
    cg, sg = c[:, idx], s[:, idx]
    cos_t = jnp.where(first | second, cg, 1.0)
    sa_t = jnp.where(second, sg, 0.0)
    sb_t = jnp.where(first, -sg, 0.0)
    return cos_t, sa_t, sb_t


def _attn_prompt_body(lam_init, blk, lq1_ref, lk1_ref, lq2_ref, lk2_ref, sg_ref,
                      q_ref, k_ref, v_ref, o_ref, s_a, s_b, mx_a, mx_b, acc_sc):
    n_q = q_ref.shape[1] // blk
    buf_a, buf_b = (s_a, mx_a), (s_b, mx_b)
    lane = lax.broadcasted_iota(jnp.int32, (blk, LANES), 1)
    r2 = lax.broadcasted_iota(jnp.int32, (2 * blk, blk), 0)
    c2 = lax.broadcasted_iota(jnp.int32, (2 * blk, blk), 1)
    causal = c2 <= jnp.where(r2 >= blk, r2 - blk, r2)
    lam = _diff_lambda(lq1_ref[...], lk1_ref[...], lq2_ref[...], lk2_ref[...], lam_init)
    gain = sg_ref[...] * (1.0 - lam_init)

    def fold(s):
        out = s[:, :LANES]
        for i in range(1, blk // LANES):
            out = jnp.maximum(out, s[:, i * LANES:(i + 1) * LANES])
        return out

    def stacked_q(qi):
        q = q_ref[0, pl.ds(pl.multiple_of(qi * blk, blk), blk), :]
        zero = jnp.zeros_like(q)
        return jnp.concatenate([jnp.where(lane < DA_HEAD_DIM, q, zero),
                                jnp.where(lane >= DA_HEAD_DIM, q, zero)], axis=0)

    def score_tile(q_st, buf, j):
        s_ref, mx_ref = buf
        s = _nt_dot(q_st, k_ref[0, pl.ds(pl.multiple_of(j * blk, blk), blk), :])
        s_ref[j] = s
        mx_ref[...] = jnp.maximum(mx_ref[...], fold(s))

    def diag_tile(qi, buf):
        s_ref, mx_ref = buf
        q0 = pl.multiple_of(qi * blk, blk)
        s = jnp.where(causal, _nt_dot(stacked_q(qi), k_ref[0, pl.ds(q0, blk), :]), NEG_BIG)
        s_ref[qi] = s
        m = jnp.max(jnp.maximum(mx_ref[...], fold(s)), axis=1, keepdims=True)
        mx_ref[...] = jnp.broadcast_to(m, (2 * blk, LANES))

    def prob_tile(buf, j):
        s_ref, mx_ref = buf
        mb = mx_ref[...]
        p = jnp.exp2(s_ref[j] - jnp.concatenate([mb] * (blk // LANES), axis=1)).astype(BF16)
        acc_sc[...] += jnp.dot(p, v_ref[0, pl.ds(pl.multiple_of(j * blk, blk), blk), :],
                               preferred_element_type=F32)

    def finish(qi):
        acc = acc_sc[...]
        o = (acc[:blk, :DA_V_DIM] / acc[:blk, DA_V_DIM:]
             - lam * (acc[blk:, :DA_V_DIM] / acc[blk:, DA_V_DIM:]))
        o = _rmsnorm_rows(o, gain, SUBLN_EPS)
        o_ref[0, pl.ds(pl.multiple_of(qi * blk, blk), blk), :] = o.astype(o_ref.dtype)

    neg = jnp.full((2 * blk, LANES), NEG_BIG, F32)

    def fused_block(qi, cur, nxt):
        q_next = stacked_q(qi + 1)
        nxt[1][...] = neg
        acc_sc[...] = jnp.zeros(acc_sc.shape, F32)

        def tile(j):
            prob_tile(cur, j)
            score_tile(q_next, nxt, j)

        def two_tiles(j2, c):
            tile(2 * j2)
            tile(2 * j2 + 1)
            return c

        n_t = qi + 1
        lax.fori_loop(0, n_t // 2, two_tiles, 0)

        @pl.when(n_t % 2 == 1)
        def _():
            tile(n_t - 1)

        finish(qi)
        diag_tile(qi + 1, nxt)

    mx_a[...] = neg
    diag_tile(0, buf_a)

    def pair(i2, carry):
        fused_block(2 * i2, buf_a, buf_b)
        fused_block(2 * i2 + 1, buf_b, buf_a)
        return carry

    n_fused = n_q - 1
    lax.fori_loop(0, n_fused // 2, pair, 0)
    if n_fused % 2:
        fused_block(n_fused - 1, buf_a, buf_b)
    last = buf_b if n_fused % 2 else buf_a
    acc_sc[...] = jnp.zeros(acc_sc.shape, F32)

    def tail(j2, c):
        prob_tile(last, 2 * j2)
        prob_tile(last, 2 * j2 + 1)
        return c

    lax.fori_loop(0, n_q // 2, tail, 0)
    if n_q % 2:
        prob_tile(last, n_q - 1)
    finish(n_q - 1)


def _attn_prompt(q, kb, vb1, lams, sg, lam_init, blk):
    b, s, _ = q.shape
    small = pl.BlockSpec((1, DA_HEAD_DIM), lambda bi, h: (0, 0))
    per_head = pl.BlockSpec((1, s, LANES), lambda bi, h: (bi, 0, h))
    return pl.pallas_call(
        functools.partial(_attn_prompt_body, lam_init, blk),
        grid=(b, DA_HEADS),
        in_specs=[small, small, small, small,
                  pl.BlockSpec((1, DA_V_DIM), lambda bi, h: (0, 0)),
                  per_head, per_head,
                  pl.BlockSpec((1, s, 2 * DA_V_DIM), lambda bi, h: (bi, 0, h))],
        out_specs=per_head,
        out_shape=jax.ShapeDtypeStruct((b, s, DA_WIDTH), BF16),
        scratch_shapes=[pltpu.VMEM((s // blk, 2 * blk, blk), F32),
                        pltpu.VMEM((s // blk, 2 * blk, blk), F32),
                        pltpu.VMEM((2 * blk, LANES), F32),
                        pltpu.VMEM((2 * blk, LANES), F32),
                        pltpu.VMEM((2 * blk, 2 * DA_V_DIM), F32)],
        compiler_params=_cparams(("parallel", "parallel")),
        name="attn_prompt",
    )(*lams, sg, q, kb, vb1)


def _attn_sample_body(lam_init, n_tok, pages_per_step, pt_ref, lq1_ref, lk1_ref, lq2_ref, lk2_ref,
                      sg_ref, q_ref, kn_ref, vn_ref, *rest):
    kp = rest[:pages_per_step]
    vp = rest[pages_per_step:2 * pages_per_step]
    o_ref = rest[2 * pages_per_step]
    m_sc, l_sc, a_sc = rest[2 * pages_per_step + 1:]
    j = pl.program_id(1)
    nrow = DA_HEADS * 2 * n_tok
    ncol = PAGE_SIZE * DA_HEADS

    @pl.when(j == 0)
    def _():
        m_sc[...] = jnp.full(m_sc.shape, NEG_BIG, F32)
        l_sc[...] = jnp.zeros(l_sc.shape, F32)
        a_sc[...] = jnp.zeros(a_sc.shape, F32)

    q = q_ref[0]
    row = lax.broadcasted_iota(jnp.int32, (nrow, ncol), 0)
    colk = lax.broadcasted_iota(jnp.int32, (nrow, ncol), 1)
    same_head = (colk % DA_HEADS) == (row // (2 * n_tok))

    def update(s_list, v_list):
        m = m_sc[...]
        mn = m
        for s in s_list:
            mn = jnp.maximum(mn, jnp.max(s, axis=-1, keepdims=True))
        alpha = jnp.exp2(m - mn)
        l = alpha * l_sc[...]
        a = alpha * a_sc[...]
        for s, v in zip(s_list, v_list):
            p = jnp.exp2(s - mn)
            l = l + jnp.sum(p, axis=-1, keepdims=True)
            a = a + jnp.dot(p.astype(BF16), v, preferred_element_type=F32)
        m_sc[...] = mn
        l_sc[...] = l
        a_sc[...] = a

    s_list, v_list = [], []
    for i in range(pages_per_step):
        kb = kp[i][0, 0].astype(BF16)
        s_list.append(jnp.where(same_head, _nt_dot(q, kb), NEG_BIG))
        v_list.append(vp[i][0, 0].astype(BF16))
    update(s_list, v_list)

    @pl.when(j == pl.num_programs(1) - 1)
    def _():
        rown = lax.broadcasted_iota(jnp.int32, (nrow, kn_ref.shape[1]), 0)
        coln = lax.broadcasted_iota(jnp.int32, (nrow, kn_ref.shape[1]), 1)
        ok = ((coln < n_tok * DA_HEADS) & ((coln % DA_HEADS) == (rown // (2 * n_tok)))
              & ((coln // DA_HEADS) <= (rown % n_tok)))
        s = jnp.where(ok, _nt_dot(q, kn_ref[0]), NEG_BIG)
        update([s], [vn_ref[0]])
        a = a_sc[...] / l_sc[...]
        lam = _diff_lambda(lq1_ref[...], lk1_ref[...], lq2_ref[...], lk2_ref[...], lam_init)
        for h in range(DA_HEADS):
            base = h * 2 * n_tok
            o = a[base:base + n_tok] - lam * a[base + n_tok:base + 2 * n_tok]
            o = _rmsnorm_rows(o, sg_ref[...], SUBLN_EPS) * (1.0 - lam_init)
            o_ref[0, :, h * LANES:(h + 1) * LANES] = o


def _attn_sample(layer, q_rows, kn, vn, cache_k, cache_v, page_table, lams, sg, lam_init,
                 n_tok, pages_per_step):
    db, nrow, _ = q_rows.shape
    n_pages = page_table.shape[1]
    assert n_pages % pages_per_step == 0
    pt = page_table.reshape(-1)
    small = pl.BlockSpec((1, DA_HEAD_DIM), lambda b, j, p: (0, 0))
    per_seq = lambda b, j, p: (b, 0, 0)

    def page_spec(i):
        return pl.BlockSpec(
            (1, 1, PAGE_SIZE * DA_HEADS, LANES),
            lambda b, j, p: (layer, p[b * n_pages + j * pages_per_step + i], 0, 0))

    pages = [page_spec(i) for i in range(pages_per_step)]
    grid_spec = pltpu.PrefetchScalarGridSpec(
        num_scalar_prefetch=1,
        grid=(db, n_pages // pages_per_step),
        in_specs=[small, small, small, small,
                  pl.BlockSpec((1, DA_V_DIM), lambda b, j, p: (0, 0)),
                  pl.BlockSpec((1, nrow, LANES), per_seq),
                  pl.BlockSpec((1,) + kn.shape[1:], per_seq),
                  pl.BlockSpec((1,) + vn.shape[1:], per_seq)] + pages + pages,
        out_specs=pl.BlockSpec((1, n_tok, DA_WIDTH), per_seq),
        scratch_shapes=[pltpu.VMEM((nrow, 1), F32), pltpu.VMEM((nrow, 1), F32),
                        pltpu.VMEM((nrow, DA_V_DIM), F32)],
    )
    return pl.pallas_call(
        functools.partial(_attn_sample_body, lam_init, n_tok, pages_per_step),
        grid_spec=grid_spec,
        out_shape=jax.ShapeDtypeStruct((db, n_tok, DA_WIDTH), F32),
        compiler_params=_cparams(("parallel", "arbitrary")),
        name="attn_sample",
    )(pt, *lams, sg, q_rows, kn, vn, *([cache_k] * pages_per_step), *([cache_v] * pages_per_step))


def _mix_out_prompt_body(n_chunk, a_ref, u_ref, gv_ref, h_ref, ws_ref, bst_ref, wo_ref, gf_ref,
                         ho_ref, xn_ref, g_sc):
    r = lax.broadcasted_iota(jnp.int32, (CHUNK, CHUNK), 0)
    c = lax.broadcasted_iota(jnp.int32, (CHUNK, CHUNK), 1)
    for g in range(GM_GROUPS):
        w = jnp.where(c <= r, ws_ref[g], 0.0).astype(BF16)
        bias = bst_ref[:, g:g + 1]
        lanes = slice(g * GM_CH, (g + 1) * GM_CH)
        for ci in range(n_chunk):
            rows = slice(ci * CHUNK, (ci + 1) * CHUNK)
            m = jnp.dot(w, gv_ref[rows, lanes], preferred_element_type=F32) + bias
            g_sc[rows, lanes] = (u_ref[rows, lanes].astype(F32) * m).astype(BF16)
    out = (jnp.dot(a_ref[...], wo_ref[:DA_WIDTH, :], preferred_element_type=F32)
           + jnp.dot(g_sc[...], wo_ref[DA_WIDTH:, :], preferred_element_type=F32))
    hn = h_ref[...] + out
    ho_ref[...] = hn
    _put_rows(xn_ref, _rmsnorm_rows(hn, gf_ref[...], RMS_EPS).astype(xn_ref.dtype))


def _mix_out_sample_body(n_tok, n_seq, a_ref, u_ref, gv_ref, h_ref, ws_ref, bs_ref, wo_ref, gf_ref,
                         ho_ref, xn_ref, g_sc):
    for g in range(GM_GROUPS):
        lanes = slice(g * GM_CH, (g + 1) * GM_CH)
        for t in range(n_tok):
            m = jnp.zeros((n_seq, GM_CH), F32) + bs_ref[g:g + 1, t:t + 1]
            for s in range(t + 1):
                m = m + ws_ref[g, t:t + 1, s:s + 1] * gv_ref[s * n_seq:(s + 1) * n_seq, lanes]
            rows = slice(t * n_seq, (t + 1) * n_seq)
            g_sc[rows, lanes] = u_ref[rows, lanes] * m
    out = (jnp.dot(a_ref[...], wo_ref[:DA_WIDTH, :], preferred_element_type=F32)
           + jnp.dot(g_sc[...].astype(BF16), wo_ref[DA_WIDTH:, :], preferred_element_type=F32))
    hn = h_ref[...] + out
    ho_ref[...] = hn
    _put_rows(xn_ref, _rmsnorm_rows(hn, gf_ref[...], RMS_EPS).astype(xn_ref.dtype))


def _mix_out(a, u, gv, h, ws, bs, wo_bf, gf, tm, sample_shape=None, tile_rows=False):
    m, d = h.shape
    row = lambda i: (i, 0)
    tpr = _tiles_per_row(d) if tile_rows else 1
    fixed2 = lambda i: (0, 0)
    fixed3 = lambda i: (0, 0, 0)
    if sample_shape is None:
        body = functools.partial(_mix_out_prompt_body, tm // CHUNK)
        bias = bs.T
        g_dtype = BF16
    else:
        body = functools.partial(_mix_out_sample_body, *sample_shape)
        bias = bs
        g_dtype = F32
    return pl.pallas_call(
        body,
        grid=(m // tm,),
        in_specs=[pl.BlockSpec((tm, DA_WIDTH), row),
                  pl.BlockSpec((tm, GM_WIDTH), row),
                  pl.BlockSpec((tm, GM_WIDTH), row),
                  pl.BlockSpec((tm, d), row),
                  pl.BlockSpec(ws.shape, fixed3),
                  pl.BlockSpec(bias.shape, fixed2),
                  pl.BlockSpec(wo_bf.shape, fixed2),
                  pl.BlockSpec((1, d), fixed2)],
        out_specs=[pl.BlockSpec((tm, d), row), pl.BlockSpec((tm * tpr, d // tpr), row)],
        out_shape=[jax.ShapeDtypeStruct((m, d), F32),
                   jax.ShapeDtypeStruct((m * tpr, d // tpr), F32 if tile_rows else BF16)],
        scratch_shapes=[pltpu.VMEM((tm, GM_WIDTH), g_dtype)],
        compiler_params=_cparams(("parallel",)),
        name="mix_out",
    )(a, u, gv, h, ws, bias, wo_bf, gf)


def _swiglu(x, wg_ref, wu_ref, wd_ref):
    g = jnp.dot(x, wg_ref[...], preferred_element_type=F32)
    u = jnp.dot(x, wu_ref[...], preferred_element_type=F32)
    hid = (g * jax.nn.sigmoid(g) * u).astype(BF16)
    return jnp.dot(hid, wd_ref[...], preferred_element_type=F32)


def _ffn_body(xn_ref, h_ref, wg_ref, wu_ref, wd_ref, o_ref):
    o_ref[...] = h_ref[...] + _swiglu(xn_ref[...], wg_ref, wu_ref, wd_ref)


def _ffn(xn, h, wg, wu, wd, tm):
    m, d = h.shape
    ff = wg.shape[1]
    resident = dict(pipeline_mode=pl.Buffered(1))
    return pl.pallas_call(
        _ffn_body,
        grid=(m // tm,),
        in_specs=[pl.BlockSpec((tm, d), lambda i: (i, 0)),
                  pl.BlockSpec((tm, d), lambda i: (i, 0)),
                  pl.BlockSpec((d, ff), lambda i: (0, 0), **resident),
                  pl.BlockSpec((d, ff), lambda i: (0, 0), **resident),
                  pl.BlockSpec((ff, d), lambda i: (0, 0), **resident)],
        out_specs=pl.BlockSpec((tm, d), lambda i: (i, 0)),
        out_shape=jax.ShapeDtypeStruct((m, d), F32),
        compiler_params=_cparams(("parallel",)),
        name="ffn",
    )(xn, h, wg, wu, wd)


R_E1, R_E2, R_G1, R_G2, R_RANK1, R_RANK2 = range(6)


def _router_body(n_exp, xn_ref, wr_ref, base_ref, o_ref, cnt_ref, cnt_sc):
    i = pl.program_id(0)

    @pl.when(i == 0)
    def _():
        cnt_sc[...] = base_ref[...]

    d = wr_ref.shape[0]
    tm = xn_ref.shape[0] // _tiles_per_row(d)
    x, w = _get_rows(xn_ref, tm, d), wr_ref[...]
    x_hi, w_hi = x.astype(BF16), w.astype(BF16)
    x_lo = (x - x_hi.astype(F32)).astype(BF16)
    w_lo = (w - w_hi.astype(F32)).astype(BF16)
    logits = (jnp.dot(x_hi, w_hi, preferred_element_type=F32)
              + (jnp.dot(x_hi, w_lo, preferred_element_type=F32)
                 + jnp.dot(x_lo, w_hi, preferred_element_type=F32)))
    col = lax.broadcasted_iota(jnp.int32, logits.shape, 1)
    logits = jnp.where(col < n_exp, logits, -jnp.inf)
    big = jnp.int32(LANES)
    v1 = jnp.max(logits, axis=-1, keepdims=True)
    i1 = jnp.min(jnp.where(logits == v1, col, big), axis=-1, keepdims=True)
    rest = jnp.where(col == i1, -jnp.inf, logits)
    v2 = jnp.max(rest, axis=-1, keepdims=True)
    i2 = jnp.min(jnp.where(rest == v2, col, big), axis=-1, keepdims=True)
    e = jnp.exp(v2 - v1)
    g1 = 1.0 / (1.0 + e)
    g2 = e / (1.0 + e)
    hit = (col == i1) | (col == i2)
    r = lax.broadcasted_iota(jnp.int32, (tm, tm), 0)
    c = lax.broadcasted_iota(jnp.int32, (tm, tm), 1)
    below = jnp.where(c < r, 1.0, 0.0).astype(BF16)
    prefix = jnp.dot(below, jnp.where(hit, 1.0, 0.0).astype(BF16), preferred_element_type=F32)
    rank = prefix + cnt_sc[...]
    r1 = jnp.sum(jnp.where(col == i1, rank, 0.0), axis=-1, keepdims=True)
    r2 = jnp.sum(jnp.where(col == i2, rank, 0.0), axis=-1, keepdims=True)
    cnt_sc[...] += jnp.sum(jnp.where(hit, 1.0, 0.0), axis=0, keepdims=True)
    rec = jnp.zeros(logits.shape, F32)
    for idx, val in ((R_E1, i1.astype(F32)), (R_E2, i2.astype(F32)), (R_G1, g1), (R_G2, g2),
                     (R_RANK1, r1), (R_RANK2, r2)):
        rec = jnp.where(col == idx, val, rec)
    o_ref[...] = rec
    cnt_ref[...] = cnt_sc[...]


def _router(xn_t, wr_pad, base_counts, n_exp, tm):
    tpr = _tiles_per_row(wr_pad.shape[0])
    m = xn_t.shape[0] // tpr
    return pl.pallas_call(
        functools.partial(_router_body, n_exp),
        grid=(m // tm,),
        in_specs=[pl.BlockSpec((tm * tpr, LANES), lambda i: (i, 0)),
                  pl.BlockSpec(wr_pad.shape, lambda i: (0, 0)),
                  pl.BlockSpec((1, LANES), lambda i: (0, 0))],
        out_specs=[pl.BlockSpec((tm, LANES), lambda i: (i, 0)),
                   pl.BlockSpec((1, LANES), lambda i: (0, 0))],
        out_shape=[jax.ShapeDtypeStruct((m, LANES), F32), jax.ShapeDtypeStruct((1, LANES), F32)],
        scratch_shapes=[pltpu.VMEM((1, LANES), F32)],
        compiler_params=_cparams(("arbitrary",)),
        name="router",
    )(xn_t, wr_pad, base_counts)


def _tile_copy(src, src_row, dst, dst_row, tpr, sem):
    return pltpu.make_async_copy(src.at[pl.ds(pl.multiple_of(src_row * tpr, tpr), tpr), :],
                                 dst.at[pl.ds(pl.multiple_of(dst_row * tpr, tpr), tpr), :], sem)


def _dispatch_body(tm, tpr, slot_ref, xn_ref, xs_in_ref, xs_ref, sem):
    del xs_in_ref

    def start(t, c):
        for k in range(TOP_K):
            _tile_copy(xn_ref, t, xs_ref, slot_ref[0, 0, TOP_K * t + k], tpr, sem).start(
                priority=k % 2)
        return c

    def wait(t, c):
        for k in range(TOP_K):
            _tile_copy(xn_ref, t, xs_ref, t, tpr, sem).wait()
        return c

    lax.fori_loop(0, tm, start, 0, unroll=DMA_UNROLL)
    lax.fori_loop(0, tm, wait, 0, unroll=DMA_UNROLL)


def _dispatch(slots, xn_t, xs_t, tm, tpr):
    m = xn_t.shape[0] // tpr
    slots3 = slots.reshape(m // tm, 1, tm * TOP_K)
    return pl.pallas_call(
        functools.partial(_dispatch_body, tm, tpr),
        grid=(m // tm,),
        in_specs=[pl.BlockSpec((1, 1, tm * TOP_K), lambda i: (i, 0, 0), memory_space=pltpu.SMEM),
                  pl.BlockSpec((tm * tpr, LANES), lambda i: (i, 0)),
                  pl.BlockSpec(memory_space=pl.ANY)],
        out_specs=pl.BlockSpec(memory_space=pl.ANY),
        out_shape=jax.ShapeDtypeStruct(xs_t.shape, xs_t.dtype),
        scratch_shapes=[pltpu.SemaphoreType.DMA(())],
        input_output_aliases={2: 0},
        compiler_params=_cparams(("arbitrary",)),
        name="moe_dispatch",
    )(slots3, xn_t, xs_t)


def _combine_body(tm, tpr, slot_cur_ref, slot_nxt_ref, rec_ref, h_ref, g_ref, ys_ref, o_ref,
                  buf, sem):
    i = pl.program_id(0)
    n = pl.num_programs(0)
    d = h_ref.shape[1]

    def fetch(slot_ref, half):
        def start(t, c):
            for k in range(TOP_K):
                _tile_copy(ys_ref, slot_ref[0, 0, TOP_K * t + k], buf.at[half, k], t, tpr,
                           sem.at[half]).start(priority=k % 2)
            return c
        lax.fori_loop(0, tm, start, 0, unroll=DMA_UNROLL)

    def wait_block(half):
        def wait(t, c):
            for k in range(TOP_K):
                _tile_copy(ys_ref, t, buf.at[half, k], t, tpr, sem.at[half]).wait()
            return c
        lax.fori_loop(0, tm, wait, 0, unroll=DMA_UNROLL)

    def step(half):
        @pl.when(i == 0)
        def _():
            fetch(slot_cur_ref, half)

        @pl.when(i + 1 < n)
        def _():
            fetch(slot_nxt_ref, 1 - half)

        wait_block(half)
        rec = rec_ref[...]
        y = (rec[:, R_G1:R_G1 + 1] * _get_rows(buf.at[half, 0], tm, d)
             + rec[:, R_G2:R_G2 + 1] * _get_rows(buf.at[half, 1], tm, d))
        o_ref[...] = _rmsnorm_rows(h_ref[...] + y, g_ref[...], RMS_EPS)

    @pl.when(i % 2 == 0)
    def _():
        step(0)

    @pl.when(i % 2 == 1)
    def _():
        step(1)


def _combine_norm(slots, rec, h, g, ys_t, tm):
    m, d = h.shape
    tpr = _tiles_per_row(d)
    n = m // tm
    slots3 = slots.reshape(n, 1, tm * TOP_K)
    row = lambda i: (i, 0)
    return pl.pallas_call(
        functools.partial(_combine_body, tm, tpr),
        grid=(n,),
        in_specs=[pl.BlockSpec((1, 1, tm * TOP_K), lambda i: (i, 0, 0), memory_space=pltpu.SMEM),
                  pl.BlockSpec((1, 1, tm * TOP_K), lambda i: (jnp.minimum(i + 1, n - 1), 0, 0),
                               memory_space=pltpu.SMEM),
                  pl.BlockSpec((tm, LANES), row),
                  pl.BlockSpec((tm, d), row),
                  pl.BlockSpec((1, d), lambda i: (0, 0)),
                  pl.BlockSpec(memory_space=pl.ANY)],
        out_specs=pl.BlockSpec((tm, d), row),
        out_shape=jax.ShapeDtypeStruct((m, d), F32),
        scratch_shapes=[pltpu.VMEM((2, TOP_K, tm * tpr, LANES), F32),
                        pltpu.SemaphoreType.DMA((2,))],
        compiler_params=_cparams(("arbitrary",)),
        name="moe_combine_norm",
    )(slots3, slots3, rec, h, g, ys_t)


def _moe_body(tm, be_ref, nv_ref, xs_ref, wg_ref, wu_ref, wd_ref, o_ref):
    live = pl.program_id(0) < nv_ref[0]
    d = wg_ref.shape[1]

    @pl.when(live)
    def _():
        x = _get_rows(xs_ref, tm, d).astype(BF16)
        _put_rows(o_ref, _swiglu(x, wg_ref.at[0], wu_ref.at[0], wd_ref.at[0]))

    @pl.when(jnp.logical_not(live))
    def _():
        o_ref[...] = jnp.zeros(o_ref.shape, F32)


def _moe_experts(xs_t, block_expert, n_valid, wg, wu, wd, tm):
    d, ff = wg.shape[1], wg.shape[2]
    tpr = _tiles_per_row(d)
    p = xs_t.shape[0] // tpr
    rows = pl.BlockSpec((tm * tpr, LANES), lambda i, be, nv: (i, 0))
    resident = dict(pipeline_mode=pl.Buffered(1))
    grid_spec = pltpu.PrefetchScalarGridSpec(
        num_scalar_prefetch=2,
        grid=(p // tm,),
        in_specs=[rows,
                  pl.BlockSpec((1, d, ff), lambda i, be, nv: (be[i], 0, 0), **resident),
                  pl.BlockSpec((1, d, ff), lambda i, be, nv: (be[i], 0, 0), **resident),
                  pl.BlockSpec((1, ff, d), lambda i, be, nv: (be[i], 0, 0), **resident)],
        out_specs=rows,
    )
    return pl.pallas_call(
        functools.partial(_moe_body, tm),
        grid_spec=grid_spec,
        out_shape=jax.ShapeDtypeStruct(xs_t.shape, F32),
        compiler_params=_cparams(("arbitrary",)),
        name="moe_experts",
    )(block_expert, n_valid, xs_t, wg, wu, wd)


def _moe_final(groups, w_router, wg, wu, wd, final_g, tm):
    d = w_router.shape[0]
    n_exp = w_router.shape[1]
    n_tok = sum(h.shape[0] for _, h, _ in groups)
    wr_pad = jnp.zeros((d, LANES), F32).at[:, :n_exp].set(w_router)
    counts = jnp.zeros((1, LANES), F32)
    recs = []
    for xn, h, _ in groups:
        rec, counts = _router(xn, wr_pad, counts, n_exp, _pick(h.shape[0], 512))
        recs.append(rec)
    cnt = counts[0, :n_exp].astype(jnp.int32)
    padded = ((cnt + tm - 1) // tm) * tm
    group_end = jnp.cumsum(padded)
    group_start = group_end - padded
    n_slots = ((n_tok * TOP_K + n_exp * (tm - 1)) // tm) * tm
    n_blocks = n_slots // tm
    block_start = jnp.arange(n_blocks, dtype=jnp.int32) * tm
    n_valid = (group_end[-1] // tm).astype(jnp.int32)
    block_expert = jnp.sum((block_start[:, None] >= group_end[None, :]).astype(jnp.int32), axis=1)
    last_valid = jnp.sum((jnp.maximum(n_valid - 1, 0) * tm >= group_end).astype(jnp.int32))
    block_expert = jnp.minimum(jnp.where(block_start < n_valid * tm, block_expert, last_valid),
                               n_exp - 1).astype(jnp.int32)

    def slots_of(rec):
        e = rec[:, R_E1:R_E2 + 1].astype(jnp.int32)
        rank = rec[:, R_RANK1:R_RANK2 + 1].astype(jnp.int32)
        start = jnp.sum(jnp.where(e[..., None] == jnp.arange(n_exp), group_start, 0), axis=-1)
        return (start + rank).astype(jnp.int32)

    slots = [slots_of(rec) for rec in recs]
    tpr = _tiles_per_row(d)
    xs = jnp.zeros((n_slots * tpr, LANES), F32)
    for (xn, _, tb), sl in zip(groups, slots):
        xs = _dispatch(sl, xn, xs, _pick(sl.shape[0], 4 * tb), tpr)
    ys = _moe_experts(xs, block_expert, n_valid.reshape(1), wg, wu, wd, tm)
    return [_combine_norm(sl, rec, h, final_g, ys, tb)
            for (_, h, tb), sl, rec in zip(groups, slots, recs)]


def kernel(x_prompt, x_sample, cache_k, cache_v, page_table, w_in, w_out, norm_mix_g, norm_ffn_g,
           lam_q1, lam_k1, lam_q2, lam_k2, subln_g, gm_ln_g, gm_ln_b, gm_ws, gm_bs,
           ffn_wg, ffn_wu, ffn_wd, moe_router, moe_wg, moe_wu, moe_wd, final_norm_g):
    b, s, d = x_prompt.shape
    db, t = x_sample.shape[:2]
    depth = w_in.shape[0]
    n_pages = page_table.shape[1]
    past = n_pages * PAGE_SIZE
    n_p, n_s = b * s, db * t
    assert depth == 2 and s % CHUNK == 0

    plan = _tile_plan(s, n_p, n_pages)
    tm_p, blk, pages_per_step = plan["rows"], plan["attn_blk"], plan["pages_per_step"]
    to_bf16 = _to_bf16
    w_in_bf, w_out_bf = to_bf16(w_in), to_bf16(w_out)

    hp = x_prompt.reshape(n_p, d)
    hs = x_sample.transpose(1, 0, 2).reshape(n_s, d)
    tab_p = _rope_tables(jnp.arange(s, dtype=F32))
    pos_s = jnp.arange(t, dtype=F32) + jnp.float32(past)
    tab_s = _rope_tables(jnp.repeat(pos_s, db))
    ck = cache_k.reshape(cache_k.shape[0], cache_k.shape[1], PAGE_SIZE * DA_HEADS, LANES)
    cv = cache_v.reshape(cache_v.shape[0], cache_v.shape[1], PAGE_SIZE * DA_HEADS, LANES)
    lane = jnp.arange(LANES)
    comp_mask = jnp.stack([lane < DA_HEAD_DIM, lane >= DA_HEAD_DIM])
    new_rows = PAGE_SIZE
    assert t * DA_HEADS <= new_rows

    outs = {k: [] for k in ("ks", "vs", "gv")}
    y_p = y_s = k_all = v_all = None
    for l in range(depth):
        lam_init = 0.8 - 0.6 * math.exp(-0.3 * l)
        g_mix = norm_mix_g[l][None]
        g_ffn = norm_ffn_g[l][None]
        lng, lnb = gm_ln_g[l][None], gm_ln_b[l][None]
        lams = (lam_q1[l][None], lam_k1[l][None], lam_q2[l][None], lam_k2[l][None])
        sg = subln_g[l][None]

        q, k_all, v_all, kb, vb, u, gv = _proj(hp, g_mix, w_in_bf[l], tab_p, lng, lnb, tm_p, BF16,
                                               BF16, kv_stack=(l, depth, k_all, v_all))
        a = _attn_prompt(q.reshape(b, s, QK_COLS), kb.reshape(b, s, QK_COLS),
                         vb.reshape(b, s, 2 * DA_WIDTH), lams, sg, lam_init, blk)
        moe_layer = l % 2 == 1
        hp, xn_p = _mix_out(a.reshape(n_p, DA_WIDTH), u, gv, hp, gm_ws[l], gm_bs[l], w_out_bf[l],
                            g_ffn, tm_p, tile_rows=moe_layer)

        sq, sk, sv, skb, svb, su, sgv = _proj(hs, g_mix, w_in_bf[l], tab_s, lng, lnb, n_s, F32, F32)
        q5 = sq.reshape(t, db, DA_HEADS, 1, LANES).transpose(1, 2, 3, 0, 4)
        q_rows = jnp.where(comp_mask[None, None, :, None, :], q5, jnp.zeros((), BF16))
        q_rows = q_rows.reshape(db, DA_HEADS * 2 * t, LANES)
        pad = ((0, 0), (0, new_rows - t * DA_HEADS), (0, 0))
        kn = jnp.pad(skb.reshape(t, db, DA_HEADS, LANES).transpose(1, 0, 2, 3).reshape(db, t * DA_HEADS, LANES), pad)
        svb = svb.reshape(t, db, DA_HEADS, 2 * DA_V_DIM)[..., :DA_V_DIM]
        vn = jnp.pad(svb.transpose(1, 0, 2, 3).reshape(db, t * DA_HEADS, LANES), pad)
        sa = _attn_sample(l, q_rows, kn, vn, ck, cv, page_table, lams, sg, lam_init, t, pages_per_step)
        sa = sa.transpose(1, 0, 2).reshape(n_s, DA_WIDTH).astype(BF16)
        hs, xn_s = _mix_out(sa, su, sgv, hs, gm_ws[l], gm_bs[l], w_out_bf[l], g_ffn, n_s,
                            sample_shape=(t, db), tile_rows=moe_layer)

        outs["ks"].append(sk.reshape(t, db, DA_HEADS, 2 * DA_HEAD_DIM).transpose(1, 0, 2, 3))
        outs["vs"].append(sv.reshape(t, db, DA_HEADS, DA_V_DIM).transpose(1, 0, 2, 3))
        outs["gv"].append(sgv.reshape(t, db, GM_GROUPS, GM_CH).transpose(1, 0, 2, 3))

        j = l // 2
        if l % 2 == 0:
            wg, wu, wd = to_bf16(ffn_wg[j]), to_bf16(ffn_wu[j]), to_bf16(ffn_wd[j])
            hp = _ffn(xn_p, hp, wg, wu, wd, tm_p)
            hs = _ffn(xn_s, hs, wg, wu, wd, n_s)
        else:
            wg, wu, wd = to_bf16(moe_wg[j]), to_bf16(moe_wu[j]), to_bf16(moe_wd[j])
            y_p, y_s = _moe_final([(xn_p, hp, plan["combine_rows"]), (xn_s, hs, n_s)],
                                  moe_router[j], wg, wu, wd, final_norm_g[None],
                                  plan["expert_rows"])

    y_prompt = y_p.reshape(b, s, d)
    y_sample = y_s.reshape(t, db, d).transpose(1, 0, 2)
    return (y_prompt, y_sample,
            k_all.reshape(depth, b, s, DA_HEADS, 2 * DA_HEAD_DIM),
            v_all.reshape(depth, b, s, DA_HEADS, DA_V_DIM),
            jnp.stack(outs["ks"]), jnp.stack(outs["vs"]), jnp.stack(outs["gv"]))
```
